```python
import math, functools
import jax, jax.numpy as jnp
from jax import lax
import numpy as np

D_MODEL = 2048
BATCH = 16
SEQ = 2048
DEPTH = 4

GRID_W = 64
CTX_LEN = 256
HEAD_DIM = 64
A_HEADS = 8
A_VDIM = 2 * HEAD_DIM
A_WIDTH = A_HEADS * A_VDIM
A_QK_COLS = A_HEADS * 2 * HEAD_DIM
B_HEADS = 16
B_KV_HEADS = 4
B_GROUP = B_HEADS // B_KV_HEADS
B_WIDTH = B_HEADS * HEAD_DIM
B_KV_COLS = B_KV_HEADS * HEAD_DIM
WINDOW = 128
BLOCK = 128
N_BRANCH = 2
BRANCH_WIDTH = A_WIDTH
KV_COLS = A_QK_COLS + A_WIDTH + 2 * B_KV_COLS
IN_COLS = KV_COLS + A_QK_COLS + B_WIDTH + N_BRANCH * D_MODEL
FFN_DIM = 5632
N_EXPERTS = 8
TOP_K = 2
EXPERT_DIM = 5632
ROPE_THETA = 10000.0
EPS = 1e-6
N_MOD = 6
NEG_INF = -1e30

kernel_name = 'hybrid_diffattn_windowgqa_moe_dit'


def _rmsnorm(x, g):
    x32 = x.astype(jnp.float32)
    y = x32 * lax.rsqrt(jnp.mean(x32 * x32, axis=-1, keepdims=True) + EPS)
    return (y * g.astype(jnp.float32)).astype(x.dtype)


def _modulate(h, shift, scale):
    return h * (1 + scale) + shift


def _axial_angles(rows):
    t_row = jnp.broadcast_to(jnp.arange(rows, dtype=jnp.float32)[:, None], (rows, GRID_W)).reshape(-1)
    t_col = jnp.broadcast_to(jnp.arange(GRID_W, dtype=jnp.float32)[None, :], (rows, GRID_W)).reshape(-1)
    half = HEAD_DIM // 2
    inv = ROPE_THETA ** (-jnp.arange(0, half, 2, dtype=jnp.float32) / half)
    return t_row[:, None] * inv, t_col[:, None] * inv


def _rotate(x, ang):
    cos = jnp.cos(ang)[:, None, :]
    sin = jnp.sin(ang)[:, None, :]
    x1, x2 = jnp.split(x, 2, axis=-1)
    return jnp.concatenate([x1 * cos - x2 * sin, x2 * cos + x1 * sin], axis=-1)


def _rope2d(x, ang_row, ang_col):
    xr, xc = jnp.split(x.astype(jnp.float32), 2, axis=-1)
    return jnp.concatenate([_rotate(xr, ang_row), _rotate(xc, ang_col)], axis=-1).astype(x.dtype)


def _rope2d_pairs(x, ang_row, ang_col):
    b, s, h, _, d = x.shape
    return _rope2d(x.reshape(b, s, h * 2, d), ang_row, ang_col).reshape(b, s, h, 2, d)


def _heads_kv(p):
    b, L, _ = p.shape
    a_k, a_v, b_k, b_v = jnp.split(p, [A_QK_COLS, A_QK_COLS + A_WIDTH, A_QK_COLS + A_WIDTH + B_KV_COLS], axis=-1)
    return (a_k.reshape(b, L, A_HEADS, 2, HEAD_DIM), a_v.reshape(b, L, A_HEADS, A_VDIM),
            b_k.reshape(b, L, B_KV_HEADS, HEAD_DIM), b_v.reshape(b, L, B_KV_HEADS, HEAD_DIM))


def _heads_q(p):
    b, L, _ = p.shape
    a_q, b_q, gates = jnp.split(p, [A_QK_COLS, A_QK_COLS + B_WIDTH], axis=-1)
    return a_q.reshape(b, L, A_HEADS, 2, HEAD_DIM), b_q.reshape(b, L, B_HEADS, HEAD_DIM), gates


def _diff_attend(q, k, v, lam):
    s = jnp.einsum('bqhid,bkhid->bhiqk', q, k).astype(jnp.float32) * (HEAD_DIM ** -0.5)
    p = jax.nn.softmax(s, axis=-1)
    a = p[:, :, 0] - lam * p[:, :, 1]
    return jnp.einsum('bhqk,bkhe->bqhe', a.astype(v.dtype), v)


def _diff_attn_latent(q, k_all, v_all, lam):
    b, n, h, _, d = q.shape
    nb = n // BLOCK
    qb = q.reshape(b, nb, BLOCK, h, 2, d).transpose(1, 0, 2, 3, 4, 5)
    o = lax.map(lambda qi: _diff_attend(qi, k_all, v_all, lam), qb)
    return o.transpose(1, 0, 2, 3, 4).reshape(b, n, h, A_VDIM)


def _diff_post(o, g, lam_init):
    b, L = o.shape[:2]
    return (_rmsnorm(o, g) * (1.0 - lam_init)).reshape(b, L, A_WIDTH)


def _window_gqa_latent(q, k, v, k_ctx, v_ctx, sink):
    b, n, h, d = q.shape
    nb = n // BLOCK
    span = BLOCK + 2 * WINDOW
    qb = q.reshape(b, nb, BLOCK, B_KV_HEADS, B_GROUP, d)
    pad = ((0, 0), (WINDOW, WINDOW), (0, 0), (0, 0))
    idx = jnp.arange(nb)[:, None] * BLOCK + jnp.arange(span)[None, :]
    kb = jnp.pad(k, pad)[:, idx]
    vb = jnp.pad(v, pad)[:, idx]
    qpos = jnp.arange(nb)[:, None] * BLOCK + jnp.arange(BLOCK)[None, :]
    kpos = (idx - WINDOW)[:, None, :]
    allowed = (jnp.abs(qpos[:, :, None] - kpos) <= WINDOW) & (kpos >= 0) & (kpos < n)
    scale = d ** -0.5
    s_win = jnp.einsum('bnqkgd,bnskd->bnkgqs', qb, kb).astype(jnp.float32) * scale
    s_win = jnp.where(allowed[None, :, None, None], s_win, NEG_INF)
    s_ctx = jnp.einsum('bnqkgd,bckd->bnkgqc', qb, k_ctx).astype(jnp.float32) * scale
    snk = sink.astype(jnp.float32).reshape(B_KV_HEADS, B_GROUP)[:, :, None, None]
    m = jnp.maximum(jnp.maximum(s_win.max(-1, keepdims=True), s_ctx.max(-1, keepdims=True)), snk)
    p_win = jnp.exp(s_win - m)
    p_ctx = jnp.exp(s_ctx - m)
    denom = p_win.sum(-1, keepdims=True) + p_ctx.sum(-1, keepdims=True) + jnp.exp(snk - m)
    o = (jnp.einsum('bnkgqs,bnskd->bnqkgd', (p_win / denom).astype(v.dtype), vb)
         + jnp.einsum('bnkgqc,bckd->bnqkgd', (p_ctx / denom).astype(v.dtype), v_ctx))
    return o.reshape(b, n, h * d)


def _gqa_ctx(q, k, v, sink):
    b, c, h, d = q.shape
    qg = q.reshape(b, c, B_KV_HEADS, B_GROUP, d)
    s = jnp.einsum('bqkgd,bskd->bkgqs', qg, k).astype(jnp.float32) * (d ** -0.5)
    snk = jnp.broadcast_to(sink.astype(jnp.float32).reshape(B_KV_HEADS, B_GROUP)[None, :, :, None, None],
                           (b, B_KV_HEADS, B_GROUP, c, 1))
    p = jax.nn.softmax(jnp.concatenate([s, snk], axis=-1), axis=-1)[..., :-1]
    o = jnp.einsum('bkgqs,bskd->bqkgd', p.astype(v.dtype), v)
    return o.reshape(b, c, h * d)


def _merge(o_a, o_b, gates, w_branch, w_out):
    g_a, g_b = jnp.split(jax.nn.sigmoid(gates), 2, axis=-1)
    y = g_a * (o_a @ w_branch[0]) + g_b * (o_b @ w_branch[1])
    return y @ w_out


def _swiglu(h, w_in, w_out):
    g, u = jnp.split(h @ w_in, 2, axis=-1)
    return (jax.nn.silu(g) * u) @ w_out


def _moe_swiglu(h, w_router, w_in, w_out):
    logits = (h @ w_router).astype(jnp.float32)
    top_val, top_idx = lax.top_k(logits, TOP_K)
    wts = jax.nn.softmax(top_val, axis=-1)
    combine = jnp.sum(jax.nn.one_hot(top_idx, N_EXPERTS, dtype=jnp.float32) * wts[..., None], axis=-2)
    y = jnp.zeros_like(h)
    for e in range(N_EXPERTS):
        y = y + combine[..., e:e + 1].astype(h.dtype) * _swiglu(h, w_in[e], w_out[e])
    return y


def setup_inputs(seed: int = 0) -> dict:
    key = jax.random.key(seed)
    ks = jax.random.split(key, 20)
    f32 = jnp.float32
    n_dense = (DEPTH + 1) // 2
    n_moe = DEPTH // 2

    def nrm(k, shape, scale):
        return jax.random.normal(k, shape, f32) * scale

    return {
        'x': nrm(ks[0], (BATCH, SEQ, D_MODEL), 1.0),
        'c': nrm(ks[1], (BATCH, D_MODEL), 1.0),
        'ctx': nrm(ks[2], (BATCH, CTX_LEN, D_MODEL), 1.0),
        'c_ctx': nrm(ks[3], (D_MODEL,), 1.0),
        'w_ada': nrm(ks[4], (DEPTH, D_MODEL, N_MOD * D_MODEL), 0.3 * D_MODEL ** -0.5),
        'b_ada': nrm(ks[5], (DEPTH, N_MOD * D_MODEL), 0.01),
        'norm_g': 1.0 + nrm(ks[6], (DEPTH, 4, D_MODEL), 0.02),
        'w_in': nrm(ks[7], (DEPTH, D_MODEL, IN_COLS), D_MODEL ** -0.5),
        'diff_lambda': nrm(ks[8], (DEPTH, 4, HEAD_DIM), 0.1),
        'diff_subln_g': 1.0 + nrm(ks[9], (DEPTH, A_VDIM), 0.02),
        'sink_logit': nrm(ks[10], (DEPTH, B_HEADS), 0.5),
        'w_branch': nrm(ks[11], (DEPTH, N_BRANCH, BRANCH_WIDTH, D_MODEL), BRANCH_WIDTH ** -0.5),
        'w_out': nrm(ks[12], (DEPTH, D_MODEL, D_MODEL), D_MODEL ** -0.5),
        'w_ffn_in': nrm(ks[13], (n_dense, D_MODEL, 2 * FFN_DIM), D_MODEL ** -0.5),
        'w_ffn_out': nrm(ks[14], (n_dense, FFN_DIM, D_MODEL), FFN_DIM ** -0.5),
        'w_router': nrm(ks[15], (n_moe, D_MODEL, N_EXPERTS), D_MODEL ** -0.5),
        'w_exp_in': nrm(ks[16], (n_moe, N_EXPERTS, D_MODEL, 2 * EXPERT_DIM), D_MODEL ** -0.5),
        'w_exp_out': nrm(ks[17], (n_moe, N_EXPERTS, EXPERT_DIM, D_MODEL), EXPERT_DIM ** -0.5),
    }


def reference(x, c, ctx, c_ctx, w_ada, b_ada, norm_g, w_in, diff_lambda, diff_subln_g, sink_logit,
              w_branch, w_out, w_ffn_in, w_ffn_out, w_router, w_exp_in, w_exp_out):
    b, n, _ = x.shape
    rows = n // GRID_W
    ang_row, ang_col = _axial_angles(rows)
    s_c = jax.nn.silu(c)
    s_cc = jax.nn.silu(c_ctx)
    xc = ctx
    for l in range(DEPTH):
        last = l == DEPTH - 1
        lam_init = 0.8 - 0.6 * math.exp(-0.3 * l)
        m = (s_c @ w_ada[l] + b_ada[l]).reshape(b, N_MOD, D_MODEL)[:, :, None, :]
        mc = (s_cc @ w_ada[l] + b_ada[l]).reshape(N_MOD, D_MODEL)
        lq1, lk1, lq2, lk2 = diff_lambda[l].astype(jnp.float32)
        lam = jnp.exp(jnp.sum(lq1 * lk1)) - jnp.exp(jnp.sum(lq2 * lk2)) + lam_init

        h = _modulate(_rmsnorm(x, norm_g[l, 0]), m[:, 0], m[:, 1])
        hc = _modulate(_rmsnorm(xc, norm_g[l, 0]), mc[0], mc[1])
        p = h @ w_in[l]
        pc = hc @ (w_in[l, :, :KV_COLS] if last else w_in[l])
        a_k, a_v, b_k, b_v = _heads_kv(p[..., :KV_COLS])
        a_q, b_q, gates = _heads_q(p[..., KV_COLS:])
        ac_k, ac_v, bc_k, bc_v = _heads_kv(pc[..., :KV_COLS])

        k_all = jnp.concatenate([ac_k, _rope2d_pairs(a_k, ang_row, ang_col)], axis=1)
        v_all = jnp.concatenate([ac_v, a_v], axis=1)
        o_a = _diff_post(_diff_attn_latent(_rope2d_pairs(a_q, ang_row, ang_col), k_all, v_all, lam),
                         diff_subln_g[l], lam_init)
        o_b = _window_gqa_latent(_rope2d(b_q, ang_row, ang_col), _rope2d(b_k, ang_row, ang_col), b_v,
                                 bc_k, bc_v, sink_logit[l])
        y = _merge(o_a, o_b, gates, w_branch[l], w_out[l])
        x = x + m[:, 2] * _rmsnorm(y, norm_g[l, 1])

        if not last:
            ac_q, bc_q, gates_c = _heads_q(pc[..., KV_COLS:])
            oc_a = _diff_post(_diff_attend(ac_q, ac_k, ac_v, lam), diff_subln_g[l], lam_init)
            oc_b = _gqa_ctx(bc_q, bc_k, bc_v, sink_logit[l])
            yc = _merge(oc_a, oc_b, gates_c, w_branch[l], w_out[l])
            xc = xc + mc[2] * _rmsnorm(yc, norm_g[l, 1])

        if l % 2 == 0:
            ffn = functools.partial(_swiglu, w_in=w_ffn_in[l // 2], w_out=w_ffn_out[l // 2])
        else:
            ffn = functools.partial(_moe_swiglu, w_router=w_router[l // 2],
                                    w_in=w_exp_in[l // 2], w_out=w_exp_out[l // 2])
        h = _modulate(_rmsnorm(x, norm_g[l, 2]), m[:, 3], m[:, 4])
        x = x + m[:, 5] * _rmsnorm(ffn(h), norm_g[l, 3])
        if not last:
            hc = _modulate(_rmsnorm(xc, norm_g[l, 2]), mc[3], mc[4])
            xc = xc + mc[5] * _rmsnorm(ffn(hc), norm_g[l, 3])
    return x
```

```python
import functools
import math
from typing import NamedTuple

import jax
import jax.numpy as jnp
from jax import lax
from jax.experimental import pallas as pl
from jax.experimental.pallas import tpu as pltpu

HEAD_DIM = 64
GRID_W = 64
WINDOW = 128
BLOCK = 128
N_MOD = 6
ROPE_THETA = 10000.0
EPS = 1e-6
NEG_INF = -1e30
LANES = 128
SUBLANES = 8
VMEM_LIMIT_BYTES = 56 * 2**20
F32 = jnp.float32
BF16 = jnp.bfloat16
_NT_DIMS = (((1,), (1,)), ((), ()))


class Dims(NamedTuple):
    B: int
    S: int
    C: int
    D: int
    depth: int
    A_W: int
    A_H: int
    B_W: int
    B_H: int
    BKV: int
    G: int
    F: int
    E: int
    FE: int
    IN: int
    NL: int
    NC: int
    NT: int
    RB: int
    c_gA: int
    c_gB: int
    c_Aq: int
    c_Bq: int
    c_Ak: int
    c_Bk: int
    c_Bv: int
    c_Av: int


class Plan(NamedTuple):
    tm_proj: int
    tn_proj: int
    tq_a: int
    tm_merge: int
    tm_row: int
    tm_ffn: int
    tf_ffn: int
    tf_exp: int
    ch_dma: int
    tn_ada: int


def _pick(pref, *ns):
    t = pref
    while t > 1 and any(n % t for n in ns):
        t //= 2
    return t


def _make_dims(x, ctx, w_ada, w_in, sink_logit, w_branch, w_ffn_out, w_router, w_exp_out):
    B, S, D = x.shape
    C = ctx.shape[1]
    depth = w_ada.shape[0]
    A_W = w_branch.shape[2]
    B_W = A_W
    A_H = A_W // (2 * HEAD_DIM)
    B_H = sink_logit.shape[1]
    IN = w_in.shape[2]
    BKV = (IN - 3 * A_W - B_W - 2 * D) // 2
    G = B_H // (BKV // HEAD_DIM)
    assert B_H * HEAD_DIM == B_W and G % 2 == 0 and (BKV // HEAD_DIM) % 2 == 0
    assert S % GRID_W == 0 and S >= BLOCK + 2 * WINDOW and S % BLOCK == 0
    c_gA, c_gB, c_Aq = 0, D, 2 * D
    c_Bq = c_Aq + A_W
    c_Ak = c_Bq + B_W
    c_Bk = c_Ak + A_W
    c_Bv = c_Bk + BKV
    c_Av = c_Bv + BKV
    assert c_Av + A_W == IN
    RB = -(-(B + 1) // SUBLANES) * SUBLANES
    return Dims(B, S, C, D, depth, A_W, A_H, B_W, B_H, BKV, G, w_ffn_out.shape[1], w_router.shape[2],
                w_exp_out.shape[2], IN, B * S, B * C, B * S + B * C, RB,
                c_gA, c_gB, c_Aq, c_Bq, c_Ak, c_Bk, c_Bv, c_Av)


def _make_plan(d):
    return Plan(
        tm_proj=_pick(1024, d.S, d.NC),
        tn_proj=_pick(512, d.IN),
        tq_a=_pick(256, d.S, d.C),
        tm_merge=_pick(256, d.S, d.NC),
        tm_row=_pick(512, d.S, d.NC),
        tm_ffn=_pick(1024, d.NL, d.NC),
        tf_ffn=_pick(512, d.F),
        tf_exp=_pick(512, d.FE),
        ch_dma=_pick(256, d.NL, d.NC),
        tn_ada=_pick(1024, N_MOD * d.D),
    )


def _cparams(*sem):
    return pltpu.CompilerParams(dimension_semantics=sem, vmem_limit_bytes=VMEM_LIMIT_BYTES)


def _rms(x, g):
    return x * lax.rsqrt(jnp.mean(x * x, axis=-1, keepdims=True) + EPS) * g


def _pack_halves(x):
    n = x.shape[1] // 2
    lo = lax.bitcast_convert_type(x[:, :n].astype(BF16).astype(F32), jnp.int32)
    hi = lax.bitcast_convert_type(x[:, n:].astype(BF16).astype(F32), jnp.int32)
    return lax.shift_right_logical(lo, jnp.int32(16)) | (hi & jnp.int32(-65536))


def _unpack_halves(w):
    lo = lax.bitcast_convert_type(lax.shift_left(w, jnp.int32(16)), F32)
    hi = lax.bitcast_convert_type(w & jnp.int32(-65536), F32)
    return jnp.concatenate([lo, hi], axis=1)


def _mod_row(i, tm, d):
    return jnp.where(i < d.NL // tm, (i * tm) // d.S, d.B)


def _mod_spec(k, tm, d, nargs):
    if nargs == 1:
        return pl.BlockSpec((None, 1, d.D), lambda i: (_mod_row(i, tm, d) * N_MOD + k, 0, 0))
    return pl.BlockSpec((None, 1, d.D), lambda i, j: (_mod_row(i, tm, d) * N_MOD + k, 0, 0))


def _ada_kernel(s_ref, w_ref, b_ref, o_ref):
    s = s_ref[...]
    s = (s * jax.nn.sigmoid(s)).astype(BF16)
    o_ref[...] = jnp.dot(s, w_ref[...].astype(BF16), preferred_element_type=F32) + b_ref[...]


def _ada_table(cond, w_ada, b_ada, d, p):
    n6 = N_MOD * d.D
    return pl.pallas_call(
        _ada_kernel,
        grid=(d.depth, n6 // p.tn_ada),
        in_specs=[
            pl.BlockSpec((d.RB, d.D), lambda l, j: (0, 0)),
            pl.BlockSpec((None, d.D, p.tn_ada), lambda l, j: (l, 0, j)),
            pl.BlockSpec((None, 1, p.tn_ada), lambda l, j: (l, 0, j)),
        ],
        out_specs=pl.BlockSpec((None, d.RB, p.tn_ada), lambda l, j: (l, 0, j)),
        out_shape=jax.ShapeDtypeStruct((d.depth, d.RB, n6), F32),
        compiler_params=_cparams("arbitrary", "arbitrary"),
        name="ada_table",
    )(cond, w_ada, b_ada.reshape(d.depth, 1, n6))


def _rope_chunk(rc, cos, sin, low_half):
    partner = jnp.where(low_half, pltpu.roll(rc, LANES - 16, 1), pltpu.roll(rc, 16, 1))
    return rc * cos + partner * sin


def _in_proj_kernel(x_ref, g_ref, shift_ref, scale_ref, w_ref, cos_ref, sin_ref, o_ref, h_scr, *, variants):
    j = pl.program_id(1)

    @pl.when(j == 0)
    def _():
        h = _rms(x_ref[...], g_ref[...]) * (1.0 + scale_ref[...]) + shift_ref[...]
        h_scr[...] = h.astype(BF16)

    r = jnp.dot(h_scr[...], w_ref[...], preferred_element_type=F32)
    tm, tn = r.shape

    for lo, hi, kinds in variants:
        @pl.when((j >= lo) & (j < hi))
        def _(kinds=kinds):
            if not any(kinds):
                o_ref[...] = r.astype(o_ref.dtype)
                return
            cos = cos_ref[...]
            sin = sin_ref[...]
            lane = lax.broadcasted_iota(jnp.int32, (tm, LANES), 1)
            low_half = (lane % 32) < 16
            for c, kind in enumerate(kinds):
                rc = r[:, c * LANES:(c + 1) * LANES]
                if kind == 2:
                    rc = rc * (HEAD_DIM ** -0.5)
                if kind:
                    rc = _rope_chunk(rc, cos, sin, low_half)
                o_ref[:, c * LANES:(c + 1) * LANES] = rc.astype(o_ref.dtype)


def _proj_variants(d, tn):
    def kind(col):
        if d.c_Aq <= col < d.c_Ak:
            return 2
        if d.c_Ak <= col < d.c_Bv:
            return 1
        return 0
    per_tile = [tuple(kind(j * tn + c * LANES) for c in range(tn // LANES)) for j in range(d.IN // tn)]
    out, lo = [], 0
    for j in range(1, len(per_tile) + 1):
        if j == len(per_tile) or per_tile[j] != per_tile[lo]:
            out.append((lo, j, per_tile[lo]))
            lo = j
    return tuple(out)


def _in_proj(X, modl, g, w, cos_t, sin_t, d, p, n_rows):
    tm, tn = p.tm_proj, p.tn_proj
    nl_tiles, s_tiles = d.NL // tm, d.S // tm
    rope_spec = pl.BlockSpec((tm, LANES), lambda i, j: (jnp.where(i < nl_tiles, i % s_tiles, s_tiles), 0))
    return pl.pallas_call(
        functools.partial(_in_proj_kernel, variants=_proj_variants(d, tn)),
        grid=(n_rows // tm, d.IN // tn),
        in_specs=[
            pl.BlockSpec((tm, d.D), lambda i, j: (i, 0)),
            pl.BlockSpec((1, d.D), lambda i, j: (0, 0)),
            _mod_spec(0, tm, d, 2),
            _mod_spec(1, tm, d, 2),
            pl.BlockSpec((d.D, tn), lambda i, j: (0, j)),
            rope_spec,
            rope_spec,
        ],
        out_specs=pl.BlockSpec((tm, tn), lambda i, j: (i, j)),
        out_shape=jax.ShapeDtypeStruct((n_rows, d.IN), BF16),
        scratch_shapes=[pltpu.VMEM((tm, d.D), BF16)],
        compiler_params=_cparams("arbitrary", "arbitrary"),
        name="in_proj",
    )(X, g, modl, modl, w, cos_t, sin_t)


def _attn_a_kernel(*refs, has_lat, lam_init, n_ctx):
    if has_lat:
        q_ref, kc_ref, vc_ref, kl_ref, vl_ref, lam_ref, g_ref, o_ref, k_scr, v_scr = refs

        @pl.when(pl.program_id(2) == 0)
        def _():
            k_scr[:n_ctx, :] = kc_ref[...]
            k_scr[n_ctx:, :] = kl_ref[...]
            v_scr[:n_ctx, :] = vc_ref[...]
            v_scr[n_ctx:, :] = vl_ref[...]

        k, v = k_scr[...], v_scr[...]
    else:
        q_ref, kc_ref, vc_ref, lam_ref, g_ref, o_ref = refs
        k, v = kc_ref[...], vc_ref[...]

    lp = lam_ref[...]
    lam = (jnp.exp(jnp.sum(lp[0:1, :] * lp[1:2, :], axis=-1, keepdims=True))
           - jnp.exp(jnp.sum(lp[2:3, :] * lp[3:4, :], axis=-1, keepdims=True)) + lam_init)

    qf = q_ref[...].astype(F32)
    first = lax.broadcasted_iota(jnp.int32, qf.shape, 1) < HEAD_DIM
    q1 = jnp.where(first, qf, 0.0).astype(BF16)
    q2 = jnp.where(first, 0.0, qf).astype(BF16)
    s1 = lax.dot_general(q1, k, _NT_DIMS, preferred_element_type=F32)
    s2 = lax.dot_general(q2, k, _NT_DIMS, preferred_element_type=F32)
    e1 = jnp.exp(s1 - jnp.max(s1, axis=-1, keepdims=True))
    e2 = jnp.exp(s2 - jnp.max(s2, axis=-1, keepdims=True))
    w1 = 1.0 / jnp.sum(e1, axis=-1, keepdims=True)
    w2 = lam / jnp.sum(e2, axis=-1, keepdims=True)
    a = (e1 * w1 - e2 * w2).astype(BF16)
    o = jnp.dot(a, v, preferred_element_type=F32)
    o = _rms(o, g_ref[...]) * (1.0 - lam_init)
    o_ref[...] = o.astype(o_ref.dtype)


def _attn_a(P, lam_p, subln_g, lam_init, d, p, latent):
    lane_blk = lambda c: c // LANES
    kb, vb = lane_blk(d.c_Ak), lane_blk(d.c_Av)
    ctx_row = d.NL // d.C
    kern = functools.partial(_attn_a_kernel, has_lat=latent, lam_init=lam_init, n_ctx=d.C)
    small = [pl.BlockSpec((4, HEAD_DIM), lambda *_: (0, 0)), pl.BlockSpec((1, 2 * HEAD_DIM), lambda *_: (0, 0))]
    if latent:
        tq = p.tq_a
        nq = d.S // tq
        qb = lane_blk(d.c_Aq)
        return pl.pallas_call(
            kern,
            grid=(d.B, d.A_H, nq),
            in_specs=[
                pl.BlockSpec((tq, LANES), lambda b, h, i: (b * nq + i, qb + h)),
                pl.BlockSpec((d.C, LANES), lambda b, h, i: (ctx_row + b, kb + h)),
                pl.BlockSpec((d.C, LANES), lambda b, h, i: (ctx_row + b, vb + h)),
                pl.BlockSpec((d.S, LANES), lambda b, h, i: (b, kb + h)),
                pl.BlockSpec((d.S, LANES), lambda b, h, i: (b, vb + h)),
            ] + small,
            out_specs=pl.BlockSpec((tq, LANES), lambda b, h, i: (b * nq + i, h)),
            out_shape=jax.ShapeDtypeStruct((d.NL, d.A_W), BF16),
            scratch_shapes=[pltpu.VMEM((d.C + d.S, LANES), BF16), pltpu.VMEM((d.C + d.S, LANES), BF16)],
            compiler_params=_cparams("arbitrary", "arbitrary", "arbitrary"),
            name="attn_a_latent",
        )(P, P, P, P, P, lam_p, subln_g)
    qb = lane_blk(d.c_Aq)
    return pl.pallas_call(
        kern,
        grid=(d.B, d.A_H),
        in_specs=[
            pl.BlockSpec((d.C, LANES), lambda b, h: (ctx_row + b, qb + h)),
            pl.BlockSpec((d.C, LANES), lambda b, h: (ctx_row + b, kb + h)),
            pl.BlockSpec((d.C, LANES), lambda b, h: (ctx_row + b, vb + h)),
        ] + small,
        out_specs=pl.BlockSpec((d.C, LANES), lambda b, h: (b, h)),
        out_shape=jax.ShapeDtypeStruct((d.NC, d.A_W), BF16),
        compiler_params=_cparams("arbitrary", "arbitrary"),
        name="attn_a_context",
    )(P, P, P, lam_p, subln_g)


def _attn_b_kernel(*refs, has_win, G, S):
    if has_win:
        sink_ref, q_ref, kc_ref, vc_ref, kl_ref, vl_ref, o_ref = refs
    else:
        sink_ref, q_ref, kc_ref, vc_ref, o_ref = refs
    pair = pl.program_id(1)
    nh = 2 * G
    tq = q_ref.shape[0]
    qf = q_ref[...].astype(F32)
    low = lax.broadcasted_iota(jnp.int32, (tq, LANES), 1) < HEAD_DIM

    qs = []
    for c in range(G):
        qc = qf[:, c * LANES:(c + 1) * LANES]
        qr = pltpu.roll(qc, HEAD_DIM, 1)
        if c < G // 2:
            qs += [jnp.where(low, qc, 0.0), jnp.where(low, qr, 0.0)]
        else:
            qs += [jnp.where(low, 0.0, qr), jnp.where(low, 0.0, qc)]
    Q = jnp.concatenate(qs, axis=0).astype(BF16)

    kc, vc = kc_ref[...], vc_ref[...]
    s_c = lax.dot_general(Q, kc, _NT_DIMS, preferred_element_type=F32)
    if has_win:
        span = BLOCK + 2 * WINDOW
        qi = pl.program_id(2)
        start = pl.multiple_of(jnp.clip(qi * BLOCK - WINDOW, 0, S - span), BLOCK)
        kw = kl_ref[pl.ds(start, span), :]
        vw = vl_ref[pl.ds(start, span), :]
        s_w = lax.dot_general(Q, kw, _NT_DIMS, preferred_element_type=F32)
        qpos = qi * BLOCK + lax.broadcasted_iota(jnp.int32, (tq, span), 0)
        kpos = start + lax.broadcasted_iota(jnp.int32, (tq, span), 1)
        allowed = jnp.abs(qpos - kpos) <= WINDOW

    p_c, p_w = [], []
    for hh in range(nh):
        snk = sink_ref[pair * nh + hh]
        sc = s_c[hh * tq:(hh + 1) * tq]
        m = jnp.maximum(jnp.max(sc, axis=-1, keepdims=True), snk)
        if has_win:
            sw = jnp.where(allowed, s_w[hh * tq:(hh + 1) * tq], NEG_INF)
            m = jnp.maximum(m, jnp.max(sw, axis=-1, keepdims=True))
            ew = jnp.exp(sw - m)
        ec = jnp.exp(sc - m)
        den = jnp.sum(ec, axis=-1, keepdims=True) + jnp.exp(snk - m)
        if has_win:
            den = den + jnp.sum(ew, axis=-1, keepdims=True)
            p_w.append((ew * (1.0 / den)).astype(BF16))
        p_c.append((ec * (1.0 / den)).astype(BF16))
    o = jnp.dot(jnp.concatenate(p_c, axis=0), vc, preferred_element_type=F32)
    if has_win:
        o = o + jnp.dot(jnp.concatenate(p_w, axis=0), vw, preferred_element_type=F32)

    for c in range(G):
        o_lo = o[(2 * c) * tq:(2 * c + 1) * tq]
        o_hi = o[(2 * c + 1) * tq:(2 * c + 2) * tq]
        if c < G // 2:
            chunk = jnp.where(low, o_lo, pltpu.roll(o_hi, HEAD_DIM, 1))
        else:
            chunk = jnp.where(low, pltpu.roll(o_lo, HEAD_DIM, 1), o_hi)
        o_ref[:, c * LANES:(c + 1) * LANES] = chunk.astype(o_ref.dtype)


def _attn_b(P, sink, d, p, latent):
    qw = 2 * d.G * HEAD_DIM
    assert d.c_Bq % qw == 0
    n_pairs = d.BKV // LANES
    qb, kb, vb = d.c_Bq // qw, d.c_Bk // LANES, d.c_Bv // LANES
    ctx_row = d.NL // d.C
    kern = functools.partial(_attn_b_kernel, has_win=latent, G=d.G, S=d.S)
    sink_spec = pl.BlockSpec(memory_space=pltpu.SMEM)
    if latent:
        nq = d.S // BLOCK
        return pl.pallas_call(
            kern,
            grid=(d.B, n_pairs, nq),
            in_specs=[
                sink_spec,
                pl.BlockSpec((BLOCK, qw), lambda b, j, i: (b * nq + i, qb + j)),
                pl.BlockSpec((d.C, LANES), lambda b, j, i: (ctx_row + b, kb + j)),
                pl.BlockSpec((d.C, LANES), lambda b, j, i: (ctx_row + b, vb + j)),
                pl.BlockSpec((d.S, LANES), lambda b, j, i: (b, kb + j)),
                pl.BlockSpec((d.S, LANES), lambda b, j, i: (b, vb + j)),
            ],
            out_specs=pl.BlockSpec((BLOCK, qw), lambda b, j, i: (b * nq + i, j)),
            out_shape=jax.ShapeDtypeStruct((d.NL, d.B_W), BF16),
            compiler_params=_cparams("arbitrary", "arbitrary", "arbitrary"),
            name="attn_b_latent",
        )(sink, P, P, P, P, P)
    return pl.pallas_call(
        kern,
        grid=(d.B, n_pairs),
        in_specs=[
            sink_spec,
            pl.BlockSpec((d.C, qw), lambda b, j: (ctx_row + b, qb + j)),
            pl.BlockSpec((d.C, LANES), lambda b, j: (ctx_row + b, kb + j)),
            pl.BlockSpec((d.C, LANES), lambda b, j: (ctx_row + b, vb + j)),
        ],
        out_specs=pl.BlockSpec((d.C, qw), lambda b, j: (b, j)),
        out_shape=jax.ShapeDtypeStruct((d.NC, d.B_W), BF16),
        compiler_params=_cparams("arbitrary", "arbitrary"),
        name="attn_b_context",
    )(sink, P, P, P)


def _merge_kernel(oa_ref, ob_ref, ga_ref, gb_ref, wa_ref, wb_ref, wo_ref, x_ref, gate_ref, g_ref, o_ref):
    ya = jnp.dot(oa_ref[...], wa_ref[...], preferred_element_type=F32)
    yb = jnp.dot(ob_ref[...], wb_ref[...], preferred_element_type=F32)
    y = jax.nn.sigmoid(ga_ref[...].astype(F32)) * ya + jax.nn.sigmoid(gb_ref[...].astype(F32)) * yb
    z = jnp.dot(y.astype(BF16), wo_ref[...], preferred_element_type=F32)
    o_ref[...] = x_ref[...] + gate_ref[...] * _rms(z, g_ref[...])


def _merge(o_a, o_b, P, wa, wb, wo, X, modl, g, d, p, n_rows):
    tm = p.tm_merge
    const = lambda shape: pl.BlockSpec(shape, lambda i: (0, 0), pipeline_mode=pl.Buffered(1))
    return pl.pallas_call(
        _merge_kernel,
        grid=(n_rows // tm,),
        in_specs=[
            pl.BlockSpec((tm, d.A_W), lambda i: (i, 0)),
            pl.BlockSpec((tm, d.B_W), lambda i: (i, 0)),
            pl.BlockSpec((tm, d.D), lambda i: (i, d.c_gA // d.D)),
            pl.BlockSpec((tm, d.D), lambda i: (i, d.c_gB // d.D)),
            const((d.A_W, d.D)),
            const((d.B_W, d.D)),
            const((d.D, d.D)),
            pl.BlockSpec((tm, d.D), lambda i: (i, 0)),
            _mod_spec(2, tm, d, 1),
            pl.BlockSpec((1, d.D), lambda i: (0, 0)),
        ],
        out_specs=pl.BlockSpec((tm, d.D), lambda i: (i, 0)),
        out_shape=jax.ShapeDtypeStruct((n_rows, d.D), F32),
        compiler_params=_cparams("arbitrary"),
        name="merge",
    )(o_a, o_b, P, P, wa, wb, wo, X, modl, g)


def _prenorm_kernel(x_ref, g_ref, shift_ref, scale_ref, o_ref):
    h = _rms(x_ref[...], g_ref[...]) * (1.0 + scale_ref[...]) + shift_ref[...]
    o_ref[...] = h.astype(o_ref.dtype)


def _prenorm(X, modl, g, d, p, n_rows):
    tm = p.tm_row
    return pl.pallas_call(
        _prenorm_kernel,
        grid=(n_rows // tm,),
        in_specs=[
            pl.BlockSpec((tm, d.D), lambda i: (i, 0)),
            pl.BlockSpec((1, d.D), lambda i: (0, 0)),
            _mod_spec(3, tm, d, 1),
            _mod_spec(4, tm, d, 1),
        ],
        out_specs=pl.BlockSpec((tm, d.D), lambda i: (i, 0)),
        out_shape=jax.ShapeDtypeStruct((n_rows, d.D), BF16),
        compiler_params=_cparams("arbitrary"),
        name="prenorm",
    )(X, g, modl, modl)


def _ffn_kernel(te_ref, nv_ref, h_ref, wg_ref, wu_ref, wo_ref, o_ref, h_scr, acc, *, packed):
    i, j = pl.program_id(0), pl.program_id(1)
    valid = i < nv_ref[0]

    @pl.when(valid & (j == 0))
    def _():
        h_scr[...] = _unpack_halves(h_ref[...]).astype(BF16) if packed else h_ref[...]
        acc[...] = jnp.zeros_like(acc)

    @pl.when(valid)
    def _():
        h = h_scr[...]
        gt = jnp.dot(h, wg_ref[...], preferred_element_type=F32)
        up = jnp.dot(h, wu_ref[...], preferred_element_type=F32)
        a = (gt * jax.nn.sigmoid(gt) * up).astype(BF16)
        acc[...] += jnp.dot(a, wo_ref[...], preferred_element_type=F32)

    @pl.when(j == pl.num_programs(1) - 1)
    def _():
        res = jnp.where(valid, acc[...], 0.0)
        o_ref[...] = _pack_halves(res) if packed else res.astype(o_ref.dtype)


def _ffn(hs, w_in, w_out, tile_expert, n_valid, d, tm, tf, packed):
    R = hs.shape[0]
    F = w_out.shape[1]
    nj = F // tf
    hw = hs.shape[1]

    def row(i, nv):
        return jnp.minimum(i, nv[0] - 1)

    def col(i, j, nv):
        return jnp.where(i < nv[0], j, nj - 1)

    grid_spec = pltpu.PrefetchScalarGridSpec(
        num_scalar_prefetch=2,
        grid=(R // tm, nj),
        in_specs=[
            pl.BlockSpec((tm, hw), lambda i, j, te, nv: (row(i, nv), 0)),
            pl.BlockSpec((None, d.D, tf), lambda i, j, te, nv: (te[row(i, nv)], 0, col(i, j, nv))),
            pl.BlockSpec((None, d.D, tf), lambda i, j, te, nv: (te[row(i, nv)], 0, nj + col(i, j, nv))),
            pl.BlockSpec((None, tf, d.D), lambda i, j, te, nv: (te[row(i, nv)], col(i, j, nv), 0)),
        ],
        out_specs=pl.BlockSpec((tm, hw), lambda i, j, te, nv: (i, 0)),
        scratch_shapes=[pltpu.VMEM((tm, d.D), BF16), pltpu.VMEM((tm, d.D), F32)],
    )
    return pl.pallas_call(
        functools.partial(_ffn_kernel, packed=packed),
        grid_spec=grid_spec,
        out_shape=jax.ShapeDtypeStruct((R, hw), hs.dtype),
        compiler_params=_cparams("arbitrary", "arbitrary"),
        name="ffn_packed" if packed else "ffn_dense",
    )(tile_expert, n_valid, hs, w_in, w_in, w_out)


def _ffn_out_kernel(f_ref, x_ref, gate_ref, g_ref, o_ref):
    o_ref[...] = x_ref[...] + gate_ref[...] * _rms(f_ref[...].astype(F32), g_ref[...])


def _ffn_out(f, X, modl, g, d, p, n_rows):
    tm = p.tm_row
    return pl.pallas_call(
        _ffn_out_kernel,
        grid=(n_rows // tm,),
        in_specs=[
            pl.BlockSpec((tm, d.D), lambda i: (i, 0)),
            pl.BlockSpec((tm, d.D), lambda i: (i, 0)),
            _mod_spec(5, tm, d, 1),
            pl.BlockSpec((1, d.D), lambda i: (0, 0)),
        ],
        out_specs=pl.BlockSpec((tm, d.D), lambda i: (i, 0)),
        out_shape=jax.ShapeDtypeStruct((n_rows, d.D), F32),
        compiler_params=_cparams("arbitrary"),
        name="ffn_out",
    )(f, X, modl, g)


def _router_kernel(x_ref, g_ref, shift_ref, scale_ref, wr_ref, hp_ref, info_ref, wts_ref, cnt_ref, base):
    i = pl.program_id(0)

    @pl.when(i == 0)
    def _():
        base[...] = jnp.zeros_like(base)

    h = _rms(x_ref[...], g_ref[...]) * (1.0 + scale_ref[...]) + shift_ref[...]
    hp_ref[...] = _pack_halves(h)
    logits = jnp.dot(h, wr_ref[...], preferred_element_type=F32)
    tm, E = logits.shape
    lane = lax.broadcasted_iota(jnp.int32, (tm, E), 1).astype(F32)
    m1 = jnp.max(logits, axis=-1, keepdims=True)
    i1 = jnp.min(jnp.where(logits == m1, lane, float(E)), axis=-1, keepdims=True)
    rest = jnp.where(lane == i1, -jnp.inf, logits)
    m2 = jnp.max(rest, axis=-1, keepdims=True)
    i2 = jnp.min(jnp.where(rest == m2, lane, float(E)), axis=-1, keepdims=True)
    e2 = jnp.exp(m2 - m1)
    w1 = 1.0 / (1.0 + e2)
    w2 = e2 * w1

    oh1 = (lane == i1).astype(F32)
    oh2 = (lane == i2).astype(F32)
    oh = oh1 + oh2
    r_i = lax.broadcasted_iota(jnp.int32, (tm, tm), 0)
    c_i = lax.broadcasted_iota(jnp.int32, (tm, tm), 1)
    earlier = (c_i < r_i).astype(BF16)
    before = base[...] + jnp.dot(earlier, oh.astype(BF16), preferred_element_type=F32)
    rank1 = jnp.sum(oh1 * before, axis=-1, keepdims=True)
    rank2 = jnp.sum(oh2 * before, axis=-1, keepdims=True)
    new_base = base[...] + jnp.sum(oh, axis=0, keepdims=True)
    base[...] = new_base
    cnt_ref[...] = new_base.astype(jnp.int32)

    col = lax.broadcasted_iota(jnp.int32, (tm, E), 1)
    info = jnp.where(col == 0, i1, jnp.where(col == 1, i2, jnp.where(col == 2, rank1, jnp.where(col == 3, rank2, 0.0))))
    info_ref[...] = info.astype(jnp.int32)
    wts_ref[...] = jnp.where(col == 0, w1, jnp.where(col == 1, w2, 0.0))


def _router(X, modl, g, w_router, d, p, n_rows):
    tm = p.tm_row
    E = d.E
    return pl.pallas_call(
        _router_kernel,
        grid=(n_rows // tm,),
        in_specs=[
            pl.BlockSpec((tm, d.D), lambda i: (i, 0)),
            pl.BlockSpec((1, d.D), lambda i: (0, 0)),
            _mod_spec(3, tm, d, 1),
            _mod_spec(4, tm, d, 1),
            pl.BlockSpec((d.D, E), lambda i: (0, 0)),
        ],
        out_specs=[
            pl.BlockSpec((tm, d.D // 2), lambda i: (i, 0)),
            pl.BlockSpec((tm, E), lambda i: (i, 0)),
            pl.BlockSpec((tm, E), lambda i: (i, 0)),
            pl.BlockSpec((1, E), lambda i: (0, 0)),
        ],
        out_shape=[
            jax.ShapeDtypeStruct((n_rows, d.D // 2), jnp.int32),
            jax.ShapeDtypeStruct((n_rows, E), jnp.int32),
            jax.ShapeDtypeStruct((n_rows, E), F32),
            jax.ShapeDtypeStruct((1, E), jnp.int32),
        ],
        scratch_shapes=[pltpu.VMEM((1, E), F32)],
        compiler_params=_cparams("arbitrary"),
        name="router",
    )(X, g, modl, modl, w_router)


def _row_copy(src, dst, s, t, sem):
    return pltpu.make_async_copy(src.at[pl.ds(s, 1)], dst.at[pl.ds(t, 1)], sem)


def _dispatch_kernel(d1_ref, d2_ref, h_ref, init_ref, o_ref, sem, *, ch):
    del init_ref
    base = pl.program_id(0) * ch

    def issue(k, carry):
        _row_copy(h_ref, o_ref, base + k, d1_ref[0, k], sem.at[0]).start()
        _row_copy(h_ref, o_ref, base + k, d2_ref[0, k], sem.at[0]).start()
        return carry

    def drain(k, carry):
        _row_copy(h_ref, o_ref, 0, 0, sem.at[0]).wait()
        _row_copy(h_ref, o_ref, 0, 0, sem.at[0]).wait()
        return carry

    lax.fori_loop(0, ch, issue, 0)
    lax.fori_loop(0, ch, drain, 0)


def _dispatch(h_pk, d1, d2, n_slots, d, p):
    T, hw = h_pk.shape
    ch = p.ch_dma
    idx_spec = pl.BlockSpec((None, 1, ch), lambda i: (i, 0, 0), memory_space=pltpu.SMEM)
    any_spec = pl.BlockSpec(memory_space=pl.ANY)
    return pl.pallas_call(
        functools.partial(_dispatch_kernel, ch=ch),
        grid=(T // ch,),
        in_specs=[idx_spec, idx_spec, any_spec, any_spec],
        out_specs=any_spec,
        out_shape=jax.ShapeDtypeStruct((n_slots, hw), jnp.int32),
        scratch_shapes=[pltpu.SemaphoreType.DMA((1,))],
        input_output_aliases={3: 0},
        compiler_params=_cparams("arbitrary"),
        name="moe_dispatch",
    )(d1.reshape(T // ch, 1, ch), d2.reshape(T // ch, 1, ch), h_pk, jnp.zeros((n_slots, hw), jnp.int32))


def _combine_kernel(d1_ref, d2_ref, ys_ref, wts_ref, x_ref, gate_ref, g_ref, o_ref, buf_a, buf_b, sem, *, ch):
    def issue(k, carry):
        _row_copy(ys_ref, buf_a, d1_ref[0, k], k, sem.at[0]).start()
        _row_copy(ys_ref, buf_b, d2_ref[0, k], k, sem.at[0]).start()
        return carry

    def drain(k, carry):
        _row_copy(ys_ref, buf_a, 0, 0, sem.at[0]).wait()
        _row_copy(ys_ref, buf_b, 0, 0, sem.at[0]).wait()
        return carry

    lax.fori_loop(0, ch, issue, 0)
    lax.fori_loop(0, ch, drain, 0)
    w = wts_ref[...]
    f = w[:, 0:1] * _unpack_halves(buf_a[...]) + w[:, 1:2] * _unpack_halves(buf_b[...])
    o_ref[...] = x_ref[...] + gate_ref[...] * _rms(f, g_ref[...])


def _combine(ys, d1, d2, wts, X, modl, g, d, p, n_rows):
    ch = p.ch_dma
    hw = ys.shape[1]
    idx_spec = pl.BlockSpec((None, 1, ch), lambda i: (i, 0, 0), memory_space=pltpu.SMEM)
    return pl.pallas_call(
        functools.partial(_combine_kernel, ch=ch),
        grid=(n_rows // ch,),
        in_specs=[
            idx_spec, idx_spec,
            pl.BlockSpec(memory_space=pl.ANY),
            pl.BlockSpec((ch, d.E), lambda i: (i, 0)),
            pl.BlockSpec((ch, d.D), lambda i: (i, 0)),
            _mod_spec(5, ch, d, 1),
            pl.BlockSpec((1, d.D), lambda i: (0, 0)),
        ],
        out_specs=pl.BlockSpec((ch, d.D), lambda i: (i, 0)),
        out_shape=jax.ShapeDtypeStruct((n_rows, d.D), F32),
        scratch_shapes=[pltpu.VMEM((ch, hw), jnp.int32), pltpu.VMEM((ch, hw), jnp.int32),
                        pltpu.SemaphoreType.DMA((1,))],
        compiler_params=_cparams("arbitrary"),
        name="moe_combine",
    )(d1.reshape(n_rows // ch, 1, ch), d2.reshape(n_rows // ch, 1, ch), ys, wts, X, modl, g)


def _moe(X, modl, g_pre, g_post, w_router, w_in, w_out, d, p, n_rows):
    tm = p.tm_ffn
    h_pk, info, wts, counts = _router(X, modl, g_pre, w_router, d, p, n_rows)
    counts = counts[0]
    padded = (counts + tm - 1) // tm * tm
    ends = jnp.cumsum(padded)
    offs = ends - padded
    d1 = offs[info[:, 0]] + info[:, 2]
    d2 = offs[info[:, 1]] + info[:, 3]
    n_slots = 2 * n_rows + d.E * tm
    tile_end = ends // tm
    tile_expert = jnp.minimum(jnp.searchsorted(tile_end, jnp.arange(n_slots // tm, dtype=jnp.int32), side="right"),
                              d.E - 1).astype(jnp.int32)
    n_valid = tile_end[-1:].astype(jnp.int32)
    hs = _dispatch(h_pk, d1, d2, n_slots, d, p)
    ys = _ffn(hs, w_in, w_out, tile_expert, n_valid, d, tm, p.tf_exp, packed=True)
    return _combine(ys, d1, d2, wts, X, modl, g_post, d, p, n_rows)


def _dense_ffn(X, modl, g_pre, g_post, w_in, w_out, d, p, n_rows):
    tm = p.tm_ffn
    h = _prenorm(X, modl, g_pre, d, p, n_rows)
    n_tiles = n_rows // tm
    f = _ffn(h, w_in[None], w_out[None], jnp.zeros((n_tiles,), jnp.int32), jnp.full((1,), n_tiles, jnp.int32),
             d, tm, p.tf_ffn, packed=False)
    return _ffn_out(f, X, modl, g_post, d, p, n_rows)


def _rope_tables(d, tm):
    rows = d.S // GRID_W
    t_row = jnp.repeat(jnp.arange(rows, dtype=F32), GRID_W)
    t_col = jnp.tile(jnp.arange(GRID_W, dtype=F32), rows)
    half = HEAD_DIM // 2
    inv = ROPE_THETA ** (-jnp.arange(0, half, 2, dtype=F32) / half)
    ang = jnp.concatenate([t_row[:, None] * inv] * 2 + [t_col[:, None] * inv] * 2, axis=1)
    sign = jnp.tile(jnp.concatenate([-jnp.ones(half // 2, F32), jnp.ones(half // 2, F32)]), 2)
    cos = jnp.tile(jnp.cos(ang), (1, LANES // HEAD_DIM))
    sin = jnp.tile(jnp.sin(ang) * sign, (1, LANES // HEAD_DIM))
    cos = jnp.concatenate([cos, jnp.ones((tm, LANES), F32)], axis=0)
    sin = jnp.concatenate([sin, jnp.zeros((tm, LANES), F32)], axis=0)
    return cos, sin


def _permute_in_cols(w, d):
    a_k, a_v, b_k, b_v, a_q, b_q, g_a, g_b = jnp.split(
        w, [d.A_W, 2 * d.A_W, 2 * d.A_W + d.BKV, 2 * d.A_W + 2 * d.BKV, 3 * d.A_W + 2 * d.BKV,
            3 * d.A_W + 2 * d.BKV + d.B_W, 3 * d.A_W + 2 * d.BKV + d.B_W + d.D], axis=-1)
    return jnp.concatenate([g_a, g_b, a_q, b_q, a_k, b_k, b_v, a_v], axis=-1)


def kernel(x, c, ctx, c_ctx, w_ada, b_ada, norm_g, w_in, diff_lambda, diff_subln_g, sink_logit, w_branch, w_out,
           w_ffn_in, w_ffn_out, w_router, w_exp_in, w_exp_out):
    d = _make_dims(x, ctx, w_ada, w_in, sink_logit, w_branch, w_ffn_out, w_router, w_exp_out)
    p = _make_plan(d)

    X = jnp.concatenate([x.reshape(d.NL, d.D), ctx.reshape(d.NC, d.D)], axis=0)
    cond = jnp.concatenate([c, c_ctx[None, :], jnp.zeros((d.RB - d.B - 1, d.D), F32)], axis=0)
    mod_all = _ada_table(cond, w_ada, b_ada, d, p).reshape(d.depth, d.RB * N_MOD, 1, d.D)
    cos_t, sin_t = _rope_tables(d, p.tm_proj)

    w_in_b = _permute_in_cols(w_in, d).astype(BF16)
    w_branch_b = w_branch.astype(BF16)
    w_out_b = w_out.astype(BF16)
    w_ffn_in_b = w_ffn_in.astype(BF16)
    w_ffn_out_b = w_ffn_out.astype(BF16)
    w_exp_in_b = w_exp_in.astype(BF16)
    w_exp_out_b = w_exp_out.astype(BF16)

    for l in range(d.depth):
        last = l == d.depth - 1
        lam_init = 0.8 - 0.6 * math.exp(-0.3 * l)
        modl = mod_all[l]
        g = norm_g[l][:, None, :]
        n_rows = d.NL if last else d.NT

        P = _in_proj(X, modl, g[0], w_in_b[l], cos_t, sin_t, d, p, d.NT)
        o_a = _attn_a(P, diff_lambda[l], diff_subln_g[l][None, :], lam_init, d, p, latent=True)
        o_b = _attn_b(P, sink_logit[l], d, p, latent=True)
        if not last:
            o_a = jnp.concatenate([o_a, _attn_a(P, diff_lambda[l], diff_subln_g[l][None, :], lam_init, d, p, latent=False)])
            o_b = jnp.concatenate([o_b, _attn_b(P, sink_logit[l], d, p, latent=False)])
        X = _merge(o_a, o_b, P, w_branch_b[l, 0], w_branch_b[l, 1], w_out_b[l], X, modl, g[1], d, p, n_rows)
        if l % 2 == 0:
            X = _dense_ffn(X, modl, g[2], g[3], w_ffn_in_b[l // 2], w_ffn_out_b[l // 2], d, p, n_rows)
        else:
            X = _moe(X, modl, g[2], g[3], w_router[l // 2], w_exp_in_b[l // 2], w_exp_out_b[l // 2], d, p, n_rows)
    return X[:d.NL].reshape(d.B, d.S, d.D)
```

```python
import functools
import math
from typing import NamedTuple

import jax
import jax.numpy as jnp
from jax import lax
from jax.experimental import pallas as pl
from jax.experimental.pallas import tpu as pltpu

HEAD_DIM = 64
GRID_W = 64
WINDOW = 128
BLOCK = 128
N_MOD = 6
ROPE_THETA = 10000.0
EPS = 1e-6
NEG_INF = -1e30
LOG2E = math.log2(math.e)
Q_SCALE = HEAD_DIM ** -0.5 * LOG2E
LANES = 128
SUBLANES = 8
VMEM_LIMIT_BYTES = 56 * 2**20
F32 = jnp.float32
BF16 = jnp.bfloat16
_NT_DIMS = (((1,), (1,)), ((), ()))


class Dims(NamedTuple):
    B: int
    S: int
    C: int
    D: int
    depth: int
    A_W: int
    A_H: int
    B_W: int
    B_H: int
    BKV: int
    G: int
    F: int
    E: int
    FE: int
    IN: int
    NL: int
    NC: int
    NT: int
    RB: int
    c_gA: int
    c_gB: int
    c_Aq: int
    c_Bq: int
    c_Ak: int
    c_Bk: int
    c_Bv: int
    c_Av: int


class Plan(NamedTuple):
    tm_proj: int
    tn_proj: int
    tq_a: int
    tq_b: int
    tm_merge: int
    tm_row: int
    tm_ffn: int
    tf_ffn: int
    tf_exp: int
    ch_dma: int
    tn_ada: int


def _pick(pref, *ns):
    t = pref
    while t > 1 and any(n % t for n in ns):
        t //= 2
    return t


def _make_dims(x, ctx, w_ada, w_in, sink_logit, w_branch, w_ffn_out, w_router, w_exp_out):
    B, S, D = x.shape
    C = ctx.shape[1]
    depth = w_ada.shape[0]
    A_W = w_branch.shape[2]
    B_W = A_W
    A_H = A_W // (2 * HEAD_DIM)
    B_H = sink_logit.shape[1]
    IN = w_in.shape[2]
    BKV = (IN - 3 * A_W - B_W - 2 * D) // 2
    G = B_H // (BKV // HEAD_DIM)
    assert B_H * HEAD_DIM == B_W and G % 2 == 0 and (BKV // HEAD_DIM) % 2 == 0
    assert S % GRID_W == 0 and S >= BLOCK + 2 * WINDOW and S % BLOCK == 0
    c_gA, c_gB, c_Aq = 0, D, 2 * D
    c_Bq = c_Aq + A_W
    c_Ak = c_Bq + B_W
    c_Bk = c_Ak + A_W
    c_Bv = c_Bk + BKV
    c_Av = c_Bv + BKV
    assert c_Av + A_W == IN
    RB = -(-(B + 1) // SUBLANES) * SUBLANES
    return Dims(B, S, C, D, depth, A_W, A_H, B_W, B_H, BKV, G, w_ffn_out.shape[1], w_router.shape[2],
                w_exp_out.shape[2], IN, B * S, B * C, B * S + B * C, RB,
                c_gA, c_gB, c_Aq, c_Bq, c_Ak, c_Bk, c_Bv, c_Av)


def _make_plan(d):
    return Plan(
        tm_proj=_pick(1024, d.S, d.NC),
        tn_proj=_pick(512, d.IN),
        tq_a=_pick(512, d.S),
        tq_b=_pick(2 * BLOCK, d.S),
        tm_merge=_pick(256, d.S, d.NC),
        tm_row=_pick(512, d.S, d.NC),
        tm_ffn=_pick(1024, d.NL, d.NC),
        tf_ffn=_pick(512, d.F),
        tf_exp=_pick(512, d.FE),
        ch_dma=_pick(512, d.S, d.NC),
        tn_ada=_pick(1024, N_MOD * d.D),
    )


def _cparams(*sem):
    return pltpu.CompilerParams(dimension_semantics=sem, vmem_limit_bytes=VMEM_LIMIT_BYTES)


def _rms(x, g):
    return x * lax.rsqrt(jnp.mean(x * x, axis=-1, keepdims=True) + EPS) * g


def _pack_halves(x):
    n = x.shape[1] // 2
    lo = lax.bitcast_convert_type(x[:, :n].astype(BF16).astype(F32), jnp.int32)
    hi = lax.bitcast_convert_type(x[:, n:].astype(BF16).astype(F32), jnp.int32)
    return lax.shift_right_logical(lo, jnp.int32(16)) | (hi & jnp.int32(-65536))


def _unpack_halves(w):
    lo = lax.bitcast_convert_type(lax.shift_left(w, jnp.int32(16)), F32)
    hi = lax.bitcast_convert_type(w & jnp.int32(-65536), F32)
    return jnp.concatenate([lo, hi], axis=1)


def _rope(x, cos, sin):
    lane = lax.broadcasted_iota(jnp.int32, x.shape, 1)
    partner = jnp.where((lane % 32) < 16, pltpu.roll(x, LANES - 16, 1), pltpu.roll(x, 16, 1))
    return x * cos + partner * sin


def _mod_row(i, tm, d):
    return jnp.where(i < d.NL // tm, (i * tm) // d.S, d.B)


def _mod_spec(k, tm, d, nargs):
    if nargs == 1:
        return pl.BlockSpec((None, 1, d.D), lambda i: (_mod_row(i, tm, d) * N_MOD + k, 0, 0))
    return pl.BlockSpec((None, 1, d.D), lambda i, j: (_mod_row(i, tm, d) * N_MOD + k, 0, 0))


def _ada_kernel(s_ref, w_ref, b_ref, o_ref):
    s = s_ref[...]
    s = (s * jax.nn.sigmoid(s)).astype(BF16)
    o_ref[...] = jnp.dot(s, w_ref[...].astype(BF16), preferred_element_type=F32) + b_ref[...]


def _ada_table(cond, w_ada, b_ada, d, p):
    n6 = N_MOD * d.D
    return pl.pallas_call(
        _ada_kernel,
        grid=(d.depth, n6 // p.tn_ada),
        in_specs=[
            pl.BlockSpec((d.RB, d.D), lambda l, j: (0, 0)),
            pl.BlockSpec((None, d.D, p.tn_ada), lambda l, j: (l, 0, j)),
            pl.BlockSpec((None, 1, p.tn_ada), lambda l, j: (l, 0, j)),
        ],
        out_specs=pl.BlockSpec((None, d.RB, p.tn_ada), lambda l, j: (l, 0, j)),
        out_shape=jax.ShapeDtypeStruct((d.depth, d.RB, n6), F32),
        compiler_params=_cparams("arbitrary", "arbitrary"),
        name="ada_table",
    )(cond, w_ada, b_ada.reshape(d.depth, 1, n6))


def _in_proj_kernel(x_ref, g_ref, shift_ref, scale_ref, w_ref, o_ref, h_scr):
    @pl.when(pl.program_id(1) == 0)
    def _():
        h = _rms(x_ref[...], g_ref[...]) * (1.0 + scale_ref[...]) + shift_ref[...]
        h_scr[...] = h.astype(BF16)

    o_ref[...] = jnp.dot(h_scr[...], w_ref[...], preferred_element_type=F32).astype(o_ref.dtype)


def _in_proj(X, modl, g, w, l, d, p, n_rows):
    tm, tn = p.tm_proj, p.tn_proj
    return pl.pallas_call(
        _in_proj_kernel,
        grid=(n_rows // tm, d.IN // tn),
        in_specs=[
            pl.BlockSpec((tm, d.D), lambda i, j: (i, 0)),
            pl.BlockSpec((1, d.D), lambda i, j: (0, 0)),
            _mod_spec(0, tm, d, 2),
            _mod_spec(1, tm, d, 2),
            pl.BlockSpec((None, d.D, tn), lambda i, j: (l, 0, j)),
        ],
        out_specs=pl.BlockSpec((tm, tn), lambda i, j: (i, j)),
        out_shape=jax.ShapeDtypeStruct((n_rows, d.IN), BF16),
        scratch_shapes=[pltpu.VMEM((tm, d.D), BF16)],
        compiler_params=_cparams("arbitrary", "arbitrary"),
        name="in_proj",
    )(X, g, modl, modl, w)


def _diff_scores(q, k, cos, sin):
    qf = q.astype(F32) * Q_SCALE
    if cos is not None:
        qf = _rope(qf, cos, sin)
    first = lax.broadcasted_iota(jnp.int32, qf.shape, 1) < HEAD_DIM
    q1 = jnp.where(first, qf, 0.0).astype(BF16)
    q2 = jnp.where(first, 0.0, qf).astype(BF16)
    return (lax.dot_general(q1, k, _NT_DIMS, preferred_element_type=F32),
            lax.dot_general(q2, k, _NT_DIMS, preferred_element_type=F32))


def _diff_attend(s1, s2, v, lam, g, lam_init):
    e1 = jnp.exp2(s1 - jnp.max(s1, axis=-1, keepdims=True))
    e2 = jnp.exp2(s2 - jnp.max(s2, axis=-1, keepdims=True))
    l1 = jnp.sum(e1, axis=-1, keepdims=True)
    l2 = jnp.sum(e2, axis=-1, keepdims=True)
    a = (e1 - (lam * l1 / l2) * e2).astype(BF16)
    o = jnp.dot(a, v, preferred_element_type=F32) * (1.0 / l1)
    return _rms(o, g) * (1.0 - lam_init)


def _diff_lambda(lam_ref, lam_init):
    lp = lam_ref[...]
    return (jnp.exp(jnp.sum(lp[0:1, :] * lp[1:2, :], axis=-1, keepdims=True))
            - jnp.exp(jnp.sum(lp[2:3, :] * lp[3:4, :], axis=-1, keepdims=True)) + lam_init)


def _attn_a_latent_kernel(q_ref, kc_ref, vc_ref, kl_ref, vl_ref, cq_ref, sq_ref, ck_ref, sk_ref, lam_ref, g_ref,
                          o_ref, k_scr, v_scr, *, lam_init, n_ctx, n_sub):
    @pl.when(pl.program_id(2) == 0)
    def _():
        k_scr[:n_ctx, :] = kc_ref[...]
        k_scr[n_ctx:, :] = _rope(kl_ref[...].astype(F32), ck_ref[...], sk_ref[...]).astype(BF16)
        v_scr[:n_ctx, :] = vc_ref[...]
        v_scr[n_ctx:, :] = vl_ref[...]

    lam = _diff_lambda(lam_ref, lam_init)
    k, v = k_scr[...], v_scr[...]
    rows = q_ref.shape[0] // n_sub
    subs = [slice(sb * rows, (sb + 1) * rows) for sb in range(n_sub)]
    scores = [_diff_scores(q_ref[sl, :], k, cq_ref[sl, :], sq_ref[sl, :]) for sl in subs]
    for sl, (s1, s2) in zip(subs, scores):
        o_ref[sl, :] = _diff_attend(s1, s2, v, lam, g_ref[...], lam_init).astype(o_ref.dtype)


def _attn_a_context_kernel(q_ref, kc_ref, vc_ref, lam_ref, g_ref, o_ref, *, lam_init):
    s1, s2 = _diff_scores(q_ref[...], kc_ref[...], None, None)
    lam = _diff_lambda(lam_ref, lam_init)
    o_ref[...] = _diff_attend(s1, s2, vc_ref[...], lam, g_ref[...], lam_init).astype(o_ref.dtype)


def _attn_a(P, lam_p, subln_g, cos_t, sin_t, lam_init, d, p, latent):
    lane_blk = lambda c: c // LANES
    qb, kb, vb = lane_blk(d.c_Aq), lane_blk(d.c_Ak), lane_blk(d.c_Av)
    ctx_row = d.NL // d.C
    small = [pl.BlockSpec((4, HEAD_DIM), lambda *_: (0, 0)), pl.BlockSpec((1, 2 * HEAD_DIM), lambda *_: (0, 0))]
    if latent:
        tq = p.tq_a
        nq = d.S // tq
        nkeys = d.C + d.S
        q_tab = pl.BlockSpec((tq, LANES), lambda b, h, i: (i, 0))
        k_tab = pl.BlockSpec((d.S, LANES), lambda b, h, i: (0, 0))
        return pl.pallas_call(
            functools.partial(_attn_a_latent_kernel, lam_init=lam_init, n_ctx=d.C, n_sub=2),
            grid=(d.B, d.A_H, nq),
            in_specs=[
                pl.BlockSpec((tq, LANES), lambda b, h, i: (b * nq + i, qb + h)),
                pl.BlockSpec((d.C, LANES), lambda b, h, i: (ctx_row + b, kb + h)),
                pl.BlockSpec((d.C, LANES), lambda b, h, i: (ctx_row + b, vb + h)),
                pl.BlockSpec((d.S, LANES), lambda b, h, i: (b, kb + h)),
                pl.BlockSpec((d.S, LANES), lambda b, h, i: (b, vb + h)),
                q_tab, q_tab, k_tab, k_tab,
            ] + small,
            out_specs=pl.BlockSpec((tq, LANES), lambda b, h, i: (b * nq + i, h)),
            out_shape=jax.ShapeDtypeStruct((d.NL, d.A_W), BF16),
            scratch_shapes=[pltpu.VMEM((nkeys, LANES), BF16), pltpu.VMEM((nkeys, LANES), BF16)],
            compiler_params=_cparams("arbitrary", "arbitrary", "arbitrary"),
            name="attn_a_latent",
        )(P, P, P, P, P, cos_t, sin_t, cos_t, sin_t, lam_p, subln_g)
    return pl.pallas_call(
        functools.partial(_attn_a_context_kernel, lam_init=lam_init),
        grid=(d.B, d.A_H),
        in_specs=[
            pl.BlockSpec((d.C, LANES), lambda b, h: (ctx_row + b, qb + h)),
            pl.BlockSpec((d.C, LANES), lambda b, h: (ctx_row + b, kb + h)),
            pl.BlockSpec((d.C, LANES), lambda b, h: (ctx_row + b, vb + h)),
        ] + small,
        out_specs=pl.BlockSpec((d.C, LANES), lambda b, h: (b, h)),
        out_shape=jax.ShapeDtypeStruct((d.NC, d.A_W), BF16),
        compiler_params=_cparams("arbitrary", "arbitrary"),
        name="attn_a_context",
    )(P, P, P, lam_p, subln_g)


def _attn_b_kernel(*refs, has_win, G, S, n_sub):
    if has_win:
        sink_ref, q_ref, kc_ref, vc_ref, kl_ref, vl_ref, cq_ref, sq_ref, ck_ref, sk_ref, o_ref, k_scr = refs

        @pl.when(pl.program_id(2) == 0)
        def _():
            k_scr[...] = _rope(kl_ref[...].astype(F32), ck_ref[...], sk_ref[...]).astype(BF16)
    else:
        sink_ref, q_ref, kc_ref, vc_ref, o_ref = refs
    pair = pl.program_id(1)
    nh = 2 * G
    tq = q_ref.shape[0] // n_sub
    low = lax.broadcasted_iota(jnp.int32, (tq, LANES), 1) < HEAD_DIM
    kc, vc = kc_ref[...], vc_ref[...]

    for sb in range(n_sub):
        sl = slice(sb * tq, (sb + 1) * tq)
        qs = []
        for c in range(G):
            qc = q_ref[sl, c * LANES:(c + 1) * LANES].astype(F32) * Q_SCALE
            if has_win:
                qc = _rope(qc, cq_ref[sl, :], sq_ref[sl, :])
            qr = pltpu.roll(qc, HEAD_DIM, 1)
            if c < G // 2:
                qs += [jnp.where(low, qc, 0.0), jnp.where(low, qr, 0.0)]
            else:
                qs += [jnp.where(low, 0.0, qr), jnp.where(low, 0.0, qc)]
        Q = jnp.concatenate(qs, axis=0).astype(BF16)

        s_c = lax.dot_general(Q, kc, _NT_DIMS, preferred_element_type=F32)
        if has_win:
            span = BLOCK + 2 * WINDOW
            blk = pl.program_id(2) * n_sub + sb
            start = pl.multiple_of(jnp.clip(blk * BLOCK - WINDOW, 0, S - span), BLOCK)
            kw = k_scr[pl.ds(start, span), :]
            vw = vl_ref[pl.ds(start, span), :]
            s_w = lax.dot_general(Q, kw, _NT_DIMS, preferred_element_type=F32)
            qpos = blk * BLOCK + lax.broadcasted_iota(jnp.int32, (tq, span), 0)
            kpos = start + lax.broadcasted_iota(jnp.int32, (tq, span), 1)
            allowed = jnp.abs(qpos - kpos) <= WINDOW

        p_c, p_w = [], []
        for hh in range(nh):
            snk = sink_ref[pair * nh + hh] * LOG2E
            sc = s_c[hh * tq:(hh + 1) * tq]
            m = jnp.maximum(jnp.max(sc, axis=-1, keepdims=True), snk)
            if has_win:
                sw = jnp.where(allowed, s_w[hh * tq:(hh + 1) * tq], NEG_INF)
                m = jnp.maximum(m, jnp.max(sw, axis=-1, keepdims=True))
                ew = jnp.exp2(sw - m)
            ec = jnp.exp2(sc - m)
            den = jnp.sum(ec, axis=-1, keepdims=True) + jnp.exp2(snk - m)
            if has_win:
                den = den + jnp.sum(ew, axis=-1, keepdims=True)
                p_w.append((ew * (1.0 / den)).astype(BF16))
            p_c.append((ec * (1.0 / den)).astype(BF16))
        o = jnp.dot(jnp.concatenate(p_c, axis=0), vc, preferred_element_type=F32)
        if has_win:
            o = o + jnp.dot(jnp.concatenate(p_w, axis=0), vw, preferred_element_type=F32)

        for c in range(G):
            o_lo = o[(2 * c) * tq:(2 * c + 1) * tq]
            o_hi = o[(2 * c + 1) * tq:(2 * c + 2) * tq]
            if c < G // 2:
                chunk = jnp.where(low, o_lo, pltpu.roll(o_hi, HEAD_DIM, 1))
            else:
                chunk = jnp.where(low, pltpu.roll(o_lo, HEAD_DIM, 1), o_hi)
            o_ref[sl, c * LANES:(c + 1) * LANES] = chunk.astype(o_ref.dtype)


def _attn_b(P, sink, cos_t, sin_t, d, p, latent):
    qw = 2 * d.G * HEAD_DIM
    assert d.c_Bq % qw == 0
    n_pairs = d.BKV // LANES
    qb, kb, vb = d.c_Bq // qw, d.c_Bk // LANES, d.c_Bv // LANES
    ctx_row = d.NL // d.C
    sink_spec = pl.BlockSpec(memory_space=pltpu.SMEM)
    if latent:
        tq = p.tq_b
        nq = d.S // tq
        kern = functools.partial(_attn_b_kernel, has_win=True, G=d.G, S=d.S, n_sub=tq // BLOCK)
        q_tab = pl.BlockSpec((tq, LANES), lambda b, j, i: (i, 0))
        k_tab = pl.BlockSpec((d.S, LANES), lambda b, j, i: (0, 0))
        return pl.pallas_call(
            kern,
            grid=(d.B, n_pairs, nq),
            in_specs=[
                sink_spec,
                pl.BlockSpec((tq, qw), lambda b, j, i: (b * nq + i, qb + j)),
                pl.BlockSpec((d.C, LANES), lambda b, j, i: (ctx_row + b, kb + j)),
                pl.BlockSpec((d.C, LANES), lambda b, j, i: (ctx_row + b, vb + j)),
                pl.BlockSpec((d.S, LANES), lambda b, j, i: (b, kb + j)),
                pl.BlockSpec((d.S, LANES), lambda b, j, i: (b, vb + j)),
                q_tab, q_tab, k_tab, k_tab,
            ],
            out_specs=pl.BlockSpec((tq, qw), lambda b, j, i: (b * nq + i, j)),
            out_shape=jax.ShapeDtypeStruct((d.NL, d.B_W), BF16),
            scratch_shapes=[pltpu.VMEM((d.S, LANES), BF16)],
            compiler_params=_cparams("arbitrary", "arbitrary", "arbitrary"),
            name="attn_b_latent",
        )(sink, P, P, P, P, P, cos_t, sin_t, cos_t, sin_t)
    kern = functools.partial(_attn_b_kernel, has_win=False, G=d.G, S=d.S, n_sub=1)
    return pl.pallas_call(
        kern,
        grid=(d.B, n_pairs),
        in_specs=[
            sink_spec,
            pl.BlockSpec((d.C, qw), lambda b, j: (ctx_row + b, qb + j)),
            pl.BlockSpec((d.C, LANES), lambda b, j: (ctx_row + b, kb + j)),
            pl.BlockSpec((d.C, LANES), lambda b, j: (ctx_row + b, vb + j)),
        ],
        out_specs=pl.BlockSpec((d.C, qw), lambda b, j: (b, j)),
        out_shape=jax.ShapeDtypeStruct((d.NC, d.B_W), BF16),
        compiler_params=_cparams("arbitrary", "arbitrary"),
        name="attn_b_context",
    )(sink, P, P, P)


def _merge_kernel(oa_ref, ob_ref, ga_ref, gb_ref, wa_ref, wb_ref, wo_ref, x_ref, gate_ref, g_ref, o_ref):
    ya = jnp.dot(oa_ref[...], wa_ref[...], preferred_element_type=F32)
    yb = jnp.dot(ob_ref[...], wb_ref[...], preferred_element_type=F32)
    y = jax.nn.sigmoid(ga_ref[...].astype(F32)) * ya + jax.nn.sigmoid(gb_ref[...].astype(F32)) * yb
    z = jnp.dot(y.astype(BF16), wo_ref[...], preferred_element_type=F32)
    o_ref[...] = x_ref[...] + gate_ref[...] * _rms(z, g_ref[...])


def _merge(o_a, o_b, P, w_branch, w_out, l, X, modl, g, d, p, n_rows):
    tm = p.tm_merge
    once = pl.Buffered(1)
    return pl.pallas_call(
        _merge_kernel,
        grid=(n_rows // tm,),
        in_specs=[
            pl.BlockSpec((tm, d.A_W), lambda i: (i, 0)),
            pl.BlockSpec((tm, d.B_W), lambda i: (i, 0)),
            pl.BlockSpec((tm, d.D), lambda i: (i, d.c_gA // d.D)),
            pl.BlockSpec((tm, d.D), lambda i: (i, d.c_gB // d.D)),
            pl.BlockSpec((None, None, d.A_W, d.D), lambda i: (l, 0, 0, 0), pipeline_mode=once),
            pl.BlockSpec((None, None, d.B_W, d.D), lambda i: (l, 1, 0, 0), pipeline_mode=once),
            pl.BlockSpec((None, d.D, d.D), lambda i: (l, 0, 0), pipeline_mode=once),
            pl.BlockSpec((tm, d.D), lambda i: (i, 0)),
            _mod_spec(2, tm, d, 1),
            pl.BlockSpec((1, d.D), lambda i: (0, 0)),
        ],
        out_specs=pl.BlockSpec((tm, d.D), lambda i: (i, 0)),
        out_shape=jax.ShapeDtypeStruct((n_rows, d.D), F32),
        compiler_params=_cparams("arbitrary"),
        name="merge",
    )(o_a, o_b, P, P, w_branch, w_branch, w_out, X, modl, g)


def _prenorm_kernel(x_ref, g_ref, shift_ref, scale_ref, o_ref):
    h = _rms(x_ref[...], g_ref[...]) * (1.0 + scale_ref[...]) + shift_ref[...]
    o_ref[...] = h.astype(o_ref.dtype)


def _prenorm(X, modl, g, d, p, n_rows):
    tm = p.tm_row
    return pl.pallas_call(
        _prenorm_kernel,
        grid=(n_rows // tm,),
        in_specs=[
            pl.BlockSpec((tm, d.D), lambda i: (i, 0)),
            pl.BlockSpec((1, d.D), lambda i: (0, 0)),
            _mod_spec(3, tm, d, 1),
            _mod_spec(4, tm, d, 1),
        ],
        out_specs=pl.BlockSpec((tm, d.D), lambda i: (i, 0)),
        out_shape=jax.ShapeDtypeStruct((n_rows, d.D), BF16),
        compiler_params=_cparams("arbitrary"),
        name="prenorm",
    )(X, g, modl, modl)


def _ffn_kernel(te_ref, nv_ref, h_ref, wg_ref, wu_ref, wo_ref, o_ref, h_scr, acc, *, packed):
    i, j = pl.program_id(0), pl.program_id(1)
    valid = i < nv_ref[0]

    @pl.when(valid & (j == 0))
    def _():
        h_scr[...] = _unpack_halves(h_ref[...]).astype(BF16) if packed else h_ref[...]
        acc[...] = jnp.zeros_like(acc)

    @pl.when(valid)
    def _():
        h = h_scr[...]
        gt = jnp.dot(h, wg_ref[...], preferred_element_type=F32)
        up = jnp.dot(h, wu_ref[...], preferred_element_type=F32)
        a = (gt * jax.nn.sigmoid(gt) * up).astype(BF16)
        acc[...] += jnp.dot(a, wo_ref[...], preferred_element_type=F32)

    @pl.when(j == pl.num_programs(1) - 1)
    def _():
        res = jnp.where(valid, acc[...], 0.0)
        o_ref[...] = _pack_halves(res) if packed else res.astype(o_ref.dtype)


def _ffn(hs, w_in, w_out, tile_expert, n_valid, d, tm, tf, packed):
    R = hs.shape[0]
    F = w_out.shape[1]
    nj = F // tf
    hw = hs.shape[1]

    def row(i, nv):
        return jnp.minimum(i, nv[0] - 1)

    def col(i, j, nv):
        return jnp.where(i < nv[0], j, nj - 1)

    grid_spec = pltpu.PrefetchScalarGridSpec(
        num_scalar_prefetch=2,
        grid=(R // tm, nj),
        in_specs=[
            pl.BlockSpec((tm, hw), lambda i, j, te, nv: (row(i, nv), 0)),
            pl.BlockSpec((None, d.D, tf), lambda i, j, te, nv: (te[row(i, nv)], 0, col(i, j, nv))),
            pl.BlockSpec((None, d.D, tf), lambda i, j, te, nv: (te[row(i, nv)], 0, nj + col(i, j, nv))),
            pl.BlockSpec((None, tf, d.D), lambda i, j, te, nv: (te[row(i, nv)], col(i, j, nv), 0)),
        ],
        out_specs=pl.BlockSpec((tm, hw), lambda i, j, te, nv: (i, 0)),
        scratch_shapes=[pltpu.VMEM((tm, d.D), BF16), pltpu.VMEM((tm, d.D), F32)],
    )
    return pl.pallas_call(
        functools.partial(_ffn_kernel, packed=packed),
        grid_spec=grid_spec,
        out_shape=jax.ShapeDtypeStruct((R, hw), hs.dtype),
        compiler_params=_cparams("arbitrary", "arbitrary"),
        name="ffn_packed" if packed else "ffn_dense",
    )(tile_expert, n_valid, hs, w_in, w_in, w_out)


def _ffn_out_kernel(f_ref, x_ref, gate_ref, g_ref, o_ref):
    o_ref[...] = x_ref[...] + gate_ref[...] * _rms(f_ref[...].astype(F32), g_ref[...])


def _ffn_out(f, X, modl, g, d, p, n_rows):
    tm = p.tm_row
    return pl.pallas_call(
        _ffn_out_kernel,
        grid=(n_rows // tm,),
        in_specs=[
            pl.BlockSpec((tm, d.D), lambda i: (i, 0)),
            pl.BlockSpec((tm, d.D), lambda i: (i, 0)),
            _mod_spec(5, tm, d, 1),
            pl.BlockSpec((1, d.D), lambda i: (0, 0)),
        ],
        out_specs=pl.BlockSpec((tm, d.D), lambda i: (i, 0)),
        out_shape=jax.ShapeDtypeStruct((n_rows, d.D), F32),
        compiler_params=_cparams("arbitrary"),
        name="ffn_out",
    )(f, X, modl, g)


def _router_kernel(x_ref, g_ref, shift_ref, scale_ref, wr_ref, hp_ref, info_ref, wts_ref, cnt_ref, base):
    i = pl.program_id(0)

    @pl.when(i == 0)
    def _():
        base[...] = jnp.zeros_like(base)

    h = _rms(x_ref[...], g_ref[...]) * (1.0 + scale_ref[...]) + shift_ref[...]
    hp_ref[...] = _pack_halves(h)
    logits = jnp.dot(h, wr_ref[...], preferred_element_type=F32)
    tm, E = logits.shape
    lane = lax.broadcasted_iota(jnp.int32, (tm, E), 1).astype(F32)
    m1 = jnp.max(logits, axis=-1, keepdims=True)
    i1 = jnp.min(jnp.where(logits == m1, lane, float(E)), axis=-1, keepdims=True)
    rest = jnp.where(lane == i1, -jnp.inf, logits)
    m2 = jnp.max(rest, axis=-1, keepdims=True)
    i2 = jnp.min(jnp.where(rest == m2, lane, float(E)), axis=-1, keepdims=True)
    e2 = jnp.exp(m2 - m1)
    w1 = 1.0 / (1.0 + e2)
    w2 = e2 * w1

    oh1 = (lane == i1).astype(F32)
    oh2 = (lane == i2).astype(F32)
    oh = oh1 + oh2
    r_i = lax.broadcasted_iota(jnp.int32, (tm, tm), 0)
    c_i = lax.broadcasted_iota(jnp.int32, (tm, tm), 1)
    earlier = (c_i < r_i).astype(BF16)
    before = base[...] + jnp.dot(earlier, oh.astype(BF16), preferred_element_type=F32)
    rank1 = jnp.sum(oh1 * before, axis=-1, keepdims=True)
    rank2 = jnp.sum(oh2 * before, axis=-1, keepdims=True)
    new_base = base[...] + jnp.sum(oh, axis=0, keepdims=True)
    base[...] = new_base
    cnt_ref[...] = new_base.astype(jnp.int32)

    col = lax.broadcasted_iota(jnp.int32, (tm, E), 1)
    info = jnp.where(col == 0, i1, jnp.where(col == 1, i2, jnp.where(col == 2, rank1, jnp.where(col == 3, rank2, 0.0))))
    info_ref[...] = info.astype(jnp.int32)
    wts_ref[...] = jnp.where(col == 0, w1, jnp.where(col == 1, w2, 0.0))


def _router(X, modl, g, w_router, d, p, n_rows):
    tm = p.tm_row
    E = d.E
    return pl.pallas_call(
        _router_kernel,
        grid=(n_rows // tm,),
        in_specs=[
            pl.BlockSpec((tm, d.D), lambda i: (i, 0)),
            pl.BlockSpec((1, d.D), lambda i: (0, 0)),
            _mod_spec(3, tm, d, 1),
            _mod_spec(4, tm, d, 1),
            pl.BlockSpec((d.D, E), lambda i: (0, 0)),
        ],
        out_specs=[
            pl.BlockSpec((tm, d.D // 2), lambda i: (i, 0)),
            pl.BlockSpec((tm, E), lambda i: (i, 0)),
            pl.BlockSpec((tm, E), lambda i: (i, 0)),
            pl.BlockSpec((1, E), lambda i: (0, 0)),
        ],
        out_shape=[
            jax.ShapeDtypeStruct((n_rows, d.D // 2), jnp.int32),
            jax.ShapeDtypeStruct((n_rows, E), jnp.int32),
            jax.ShapeDtypeStruct((n_rows, E), F32),
            jax.ShapeDtypeStruct((1, E), jnp.int32),
        ],
        scratch_shapes=[pltpu.VMEM((1, E), F32)],
        compiler_params=_cparams("arbitrary"),
        name="router",
    )(X, g, modl, modl, w_router)


def _row_copy(src, dst, s, t, sem):
    return pltpu.make_async_copy(src.at[pl.ds(s, 1)], dst.at[pl.ds(t, 1)], sem)


def _dispatch_kernel(d1_ref, d2_ref, h_ref, init_ref, o_ref, sem, *, ch):
    del init_ref

    def issue(k, carry):
        _row_copy(h_ref, o_ref, k, d1_ref[0, k], sem.at[0]).start()
        _row_copy(h_ref, o_ref, k, d2_ref[0, k], sem.at[0]).start()
        return carry

    def drain(k, carry):
        _row_copy(h_ref, o_ref, 0, 0, sem.at[0]).wait()
        _row_copy(h_ref, o_ref, 0, 0, sem.at[0]).wait()
        return carry

    lax.fori_loop(0, ch, issue, 0, unroll=8)
    lax.fori_loop(0, ch, drain, 0, unroll=8)


def _dispatch(h_pk, d1, d2, n_slots, d, p):
    T, hw = h_pk.shape
    ch = p.ch_dma
    idx_spec = pl.BlockSpec((None, 1, ch), lambda i: (i, 0, 0), memory_space=pltpu.SMEM)
    any_spec = pl.BlockSpec(memory_space=pl.ANY)
    return pl.pallas_call(
        functools.partial(_dispatch_kernel, ch=ch),
        grid=(T // ch,),
        in_specs=[idx_spec, idx_spec, pl.BlockSpec((ch, hw), lambda i: (i, 0)), any_spec],
        out_specs=any_spec,
        out_shape=jax.ShapeDtypeStruct((n_slots, hw), jnp.int32),
        scratch_shapes=[pltpu.SemaphoreType.DMA((1,))],
        input_output_aliases={3: 0},
        compiler_params=_cparams("arbitrary"),
        name="moe_dispatch",
    )(d1.reshape(T // ch, 1, ch), d2.reshape(T // ch, 1, ch), h_pk, jnp.zeros((n_slots, hw), jnp.int32))


def _combine_kernel(d1_ref, d2_ref, ys_ref, wts_ref, x_ref, gate_ref, g_ref, o_ref, buf_a, buf_b, sem, *, ch):
    def issue(k, carry):
        _row_copy(ys_ref, buf_a, d1_ref[0, k], k, sem.at[0]).start()
        _row_copy(ys_ref, buf_b, d2_ref[0, k], k, sem.at[0]).start()
        return carry

    def drain(k, carry):
        _row_copy(ys_ref, buf_a, 0, 0, sem.at[0]).wait()
        _row_copy(ys_ref, buf_b, 0, 0, sem.at[0]).wait()
        return carry

    lax.fori_loop(0, ch, issue, 0, unroll=8)
    lax.fori_loop(0, ch, drain, 0, unroll=8)
    w = wts_ref[...]
    f = w[:, 0:1] * _unpack_halves(buf_a[...]) + w[:, 1:2] * _unpack_halves(buf_b[...])
    o_ref[...] = x_ref[...] + gate_ref[...] * _rms(f, g_ref[...])


def _combine(ys, d1, d2, wts, X, modl, g, d, p, n_rows):
    ch = p.ch_dma
    hw = ys.shape[1]
    idx_spec = pl.BlockSpec((None, 1, ch), lambda i: (i, 0, 0), memory_space=pltpu.SMEM)
    return pl.pallas_call(
        functools.partial(_combine_kernel, ch=ch),
        grid=(n_rows // ch,),
        in_specs=[
            idx_spec, idx_spec,
            pl.BlockSpec(memory_space=pl.ANY),
            pl.BlockSpec((ch, d.E), lambda i: (i, 0)),
            pl.BlockSpec((ch, d.D), lambda i: (i, 0)),
            _mod_spec(5, ch, d, 1),
            pl.BlockSpec((1, d.D), lambda i: (0, 0)),
        ],
        out_specs=pl.BlockSpec((ch, d.D), lambda i: (i, 0)),
        out_shape=jax.ShapeDtypeStruct((n_rows, d.D), F32),
        scratch_shapes=[pltpu.VMEM((ch, hw), jnp.int32), pltpu.VMEM((ch, hw), jnp.int32),
                        pltpu.SemaphoreType.DMA((1,))],
        compiler_params=_cparams("arbitrary"),
        name="moe_combine",
    )(d1.reshape(n_rows // ch, 1, ch), d2.reshape(n_rows // ch, 1, ch), ys, wts, X, modl, g)


def _moe(X, modl, g_pre, g_post, w_router, w_in, w_out, first_expert, d, p, n_rows):
    tm = p.tm_ffn
    h_pk, info, wts, counts = _router(X, modl, g_pre, w_router, d, p, n_rows)
    counts = counts[0]
    padded = (counts + tm - 1) // tm * tm
    ends = jnp.cumsum(padded)
    offs = ends - padded
    d1 = offs[info[:, 0]] + info[:, 2]
    d2 = offs[info[:, 1]] + info[:, 3]
    n_slots = 2 * n_rows + d.E * tm
    tile_end = ends // tm
    tiles = jnp.arange(n_slots // tm, dtype=jnp.int32)
    tile_expert = jnp.minimum(jnp.sum((tiles[:, None] >= tile_end[None, :]).astype(jnp.int32), axis=1), d.E - 1)
    n_valid = tile_end[-1:].astype(jnp.int32)
    hs = _dispatch(h_pk, d1, d2, n_slots, d, p)
    ys = _ffn(hs, w_in, w_out, tile_expert + first_expert, n_valid, d, tm, p.tf_exp, packed=True)
    return _combine(ys, d1, d2, wts, X, modl, g_post, d, p, n_rows)


def _dense_ffn(X, modl, g_pre, g_post, w_in, w_out, which, d, p, n_rows):
    tm = p.tm_ffn
    h = _prenorm(X, modl, g_pre, d, p, n_rows)
    n_tiles = n_rows // tm
    f = _ffn(h, w_in, w_out, jnp.full((n_tiles,), which, jnp.int32), jnp.full((1,), n_tiles, jnp.int32),
             d, tm, p.tf_ffn, packed=False)
    return _ffn_out(f, X, modl, g_post, d, p, n_rows)


def _rope_tables(d):
    rows = d.S // GRID_W
    t_row = jnp.repeat(jnp.arange(rows, dtype=F32), GRID_W)
    t_col = jnp.tile(jnp.arange(GRID_W, dtype=F32), rows)
    half = HEAD_DIM // 2
    inv = ROPE_THETA ** (-jnp.arange(0, half, 2, dtype=F32) / half)
    ang = jnp.concatenate([t_row[:, None] * inv] * 2 + [t_col[:, None] * inv] * 2, axis=1)
    sign = jnp.tile(jnp.concatenate([-jnp.ones(half // 2, F32), jnp.ones(half // 2, F32)]), 2)
    cos = jnp.tile(jnp.cos(ang), (1, LANES // HEAD_DIM))
    sin = jnp.tile(jnp.sin(ang) * sign, (1, LANES // HEAD_DIM))
    return cos, sin


def _permute_in_cols(w, d):
    a_k, a_v, b_k, b_v, a_q, b_q, g_a, g_b = jnp.split(
        w, [d.A_W, 2 * d.A_W, 2 * d.A_W + d.BKV, 2 * d.A_W + 2 * d.BKV, 3 * d.A_W + 2 * d.BKV,
            3 * d.A_W + 2 * d.BKV + d.B_W, 3 * d.A_W + 2 * d.BKV + d.B_W + d.D], axis=-1)
    return jnp.concatenate([g_a, g_b, a_q, b_q, a_k, b_k, b_v, a_v], axis=-1)


def kernel(x, c, ctx, c_ctx, w_ada, b_ada, norm_g, w_in, diff_lambda, diff_subln_g, sink_logit, w_branch, w_out,
           w_ffn_in, w_ffn_out, w_router, w_exp_in, w_exp_out):
    d = _make_dims(x, ctx, w_ada, w_in, sink_logit, w_branch, w_ffn_out, w_router, w_exp_out)
    p = _make_plan(d)

    X = jnp.concatenate([x.reshape(d.NL, d.D), ctx.reshape(d.NC, d.D)], axis=0)
    cond = jnp.concatenate([c, c_ctx[None, :], jnp.zeros((d.RB - d.B - 1, d.D), F32)], axis=0)
    mod_all = _ada_table(cond, w_ada, b_ada, d, p).reshape(d.depth, d.RB * N_MOD, 1, d.D)
    cos_t, sin_t = _rope_tables(d)

    w_in_b = _permute_in_cols(w_in, d).astype(BF16)
    w_branch_b = w_branch.astype(BF16)
    w_out_b = w_out.astype(BF16)
    w_ffn_in_b = w_ffn_in.astype(BF16)
    w_ffn_out_b = w_ffn_out.astype(BF16)
    w_exp_in_b = w_exp_in.astype(BF16).reshape(-1, d.D, 2 * d.FE)
    w_exp_out_b = w_exp_out.astype(BF16).reshape(-1, d.FE, d.D)

    for l in range(d.depth):
        last = l == d.depth - 1
        lam_init = 0.8 - 0.6 * math.exp(-0.3 * l)
        modl = mod_all[l]
        g = norm_g[l][:, None, :]
        n_rows = d.NL if last else d.NT
        subln = diff_subln_g[l][None, :]

        P = _in_proj(X, modl, g[0], w_in_b, l, d, p, d.NT)
        o_a = _attn_a(P, diff_lambda[l], subln, cos_t, sin_t, lam_init, d, p, latent=True)
        o_b = _attn_b(P, sink_logit[l], cos_t, sin_t, d, p, latent=True)
        if not last:
            o_a = jnp.concatenate([o_a, _attn_a(P, diff_lambda[l], subln, cos_t, sin_t, lam_init, d, p, latent=False)])
            o_b = jnp.concatenate([o_b, _attn_b(P, sink_logit[l], cos_t, sin_t, d, p, latent=False)])
        X = _merge(o_a, o_b, P, w_branch_b, w_out_b, l, X, modl, g[1], d, p, n_rows)
        if l % 2 == 0:
            X = _dense_ffn(X, modl, g[2], g[3], w_ffn_in_b, w_ffn_out_b, l // 2, d, p, n_rows)
        else:
            X = _moe(X, modl, g[2], g[3], w_router[l // 2], w_exp_in_b, w_exp_out_b, (l // 2) * d.E, d, p, n_rows)
    return X[:d.NL].reshape(d.B, d.S, d.D)
```

```python
import functools
import math
from typing import NamedTuple

import jax
import jax.numpy as jnp
from jax import lax
from jax.experimental import pallas as pl
from jax.experimental.pallas import tpu as pltpu

HEAD_DIM = 64
GRID_W = 64
WINDOW = 128
BLOCK = 128
N_MOD = 6
ROPE_THETA = 10000.0
EPS = 1e-6
NEG_INF = -1e30
LOG2E = math.log2(math.e)
Q_SCALE = HEAD_DIM ** -0.5 * LOG2E
LANES = 128
SUBLANES = 8
VMEM_LIMIT_BYTES = 56 * 2**20
F32 = jnp.float32
BF16 = jnp.bfloat16
_NT_DIMS = (((1,), (1,)), ((), ()))


class Dims(NamedTuple):
    B: int
    S: int
    C: int
    D: int
    depth: int
    A_W: int
    A_H: int
    B_W: int
    B_H: int
    BKV: int
    G: int
    F: int
    E: int
    FE: int
    IN: int
    NL: int
    NC: int
    NT: int
    RB: int
    c_gA: int
    c_gB: int
    c_Aq: int
    c_Bq: int
    c_Ak: int
    c_Bk: int
    c_Bv: int
    c_Av: int


class Plan(NamedTuple):
    tm_proj: int
    tn_proj: int
    tq_a: int
    tq_b: int
    tm_merge: int
    tm_row: int
    tm_ffn: int
    tf_ffn: int
    tf_exp: int
    ch_dma: int
    tn_ada: int


def _pick(pref, *ns):
    t = pref
    while t > 1 and any(n % t for n in ns):
        t //= 2
    return t


def _make_dims(x, ctx, w_ada, w_in, sink_logit, w_branch, w_ffn_out, w_router, w_exp_out):
    B, S, D = x.shape
    C = ctx.shape[1]
    depth = w_ada.shape[0]
    A_W = w_branch.shape[2]
    B_W = A_W
    A_H = A_W // (2 * HEAD_DIM)
    B_H = sink_logit.shape[1]
    IN = w_in.shape[2]
    BKV = (IN - 3 * A_W - B_W - 2 * D) // 2
    G = B_H // (BKV // HEAD_DIM)
    assert B_H * HEAD_DIM == B_W and G % 2 == 0 and (BKV // HEAD_DIM) % 2 == 0
    assert S % GRID_W == 0 and S >= BLOCK + 2 * WINDOW and S % BLOCK == 0
    c_gA, c_gB, c_Aq = 0, D, 2 * D
    c_Bq = c_Aq + A_W
    c_Ak = c_Bq + B_W
    c_Bk = c_Ak + A_W
    c_Bv = c_Bk + BKV
    c_Av = c_Bv + BKV
    assert c_Av + A_W == IN
    RB = -(-(B + 1) // SUBLANES) * SUBLANES
    return Dims(B, S, C, D, depth, A_W, A_H, B_W, B_H, BKV, G, w_ffn_out.shape[1], w_router.shape[2],
                w_exp_out.shape[2], IN, B * S, B * C, B * S + B * C, RB,
                c_gA, c_gB, c_Aq, c_Bq, c_Ak, c_Bk, c_Bv, c_Av)


def _make_plan(d):
    return Plan(
        tm_proj=_pick(1024, d.S, d.NC),
        tn_proj=_pick(512, d.IN),
        tq_a=_pick(4 * 256, d.S),
        tq_b=_pick(2 * BLOCK, d.S),
        tm_merge=_pick(256, d.S, d.NC),
        tm_row=_pick(512, d.S, d.NC),
        tm_ffn=_pick(1024, d.NL, d.NC),
        tf_ffn=_pick(512, d.F),
        tf_exp=_pick(512, d.FE),
        ch_dma=_pick(512, d.S, d.NC),
        tn_ada=_pick(1024, N_MOD * d.D),
    )


def _cparams(*sem):
    return pltpu.CompilerParams(dimension_semantics=sem, vmem_limit_bytes=VMEM_LIMIT_BYTES)


def _rms(x, g):
    return x * lax.rsqrt(jnp.mean(x * x, axis=-1, keepdims=True) + EPS) * g


def _pack_halves(x):
    n = x.shape[1] // 2
    lo = lax.bitcast_convert_type(x[:, :n].astype(BF16).astype(F32), jnp.int32)
    hi = lax.bitcast_convert_type(x[:, n:].astype(BF16).astype(F32), jnp.int32)
    return lax.shift_right_logical(lo, jnp.int32(16)) | (hi & jnp.int32(-65536))


def _unpack_halves(w):
    lo = lax.bitcast_convert_type(lax.shift_left(w, jnp.int32(16)), F32)
    hi = lax.bitcast_convert_type(w & jnp.int32(-65536), F32)
    return jnp.concatenate([lo, hi], axis=1)


def _rope(x, cos, sin):
    lane = lax.broadcasted_iota(jnp.int32, x.shape, 1)
    partner = jnp.where((lane % 32) < 16, pltpu.roll(x, LANES - 16, 1), pltpu.roll(x, 16, 1))
    return x * cos + partner * sin


def _mod_row(i, tm, d):
    return jnp.where(i < d.NL // tm, (i * tm) // d.S, d.B)


def _mod_spec(k, tm, d, nargs):
    if nargs == 1:
        return pl.BlockSpec((None, 1, d.D), lambda i: (_mod_row(i, tm, d) * N_MOD + k, 0, 0))
    return pl.BlockSpec((None, 1, d.D), lambda i, j: (_mod_row(i, tm, d) * N_MOD + k, 0, 0))


def _ada_kernel(s_ref, w_ref, b_ref, o_ref):
    s = s_ref[...]
    s = (s * jax.nn.sigmoid(s)).astype(BF16)
    o_ref[...] = jnp.dot(s, w_ref[...].astype(BF16), preferred_element_type=F32) + b_ref[...]


def _ada_table(cond, w_ada, b_ada, d, p):
    n6 = N_MOD * d.D
    return pl.pallas_call(
        _ada_kernel,
        grid=(d.depth, n6 // p.tn_ada),
        in_specs=[
            pl.BlockSpec((d.RB, d.D), lambda l, j: (0, 0)),
            pl.BlockSpec((None, d.D, p.tn_ada), lambda l, j: (l, 0, j)),
            pl.BlockSpec((None, 1, p.tn_ada), lambda l, j: (l, 0, j)),
        ],
        out_specs=pl.BlockSpec((None, d.RB, p.tn_ada), lambda l, j: (l, 0, j)),
        out_shape=jax.ShapeDtypeStruct((d.depth, d.RB, n6), F32),
        compiler_params=_cparams("arbitrary", "arbitrary"),
        name="ada_table",
    )(cond, w_ada, b_ada.reshape(d.depth, 1, n6))


def _in_proj_kernel(x_ref, g_ref, shift_ref, scale_ref, w_ref, o_ref, h_scr):
    @pl.when(pl.program_id(1) == 0)
    def _():
        h = _rms(x_ref[...], g_ref[...]) * (1.0 + scale_ref[...]) + shift_ref[...]
        h_scr[...] = h.astype(BF16)

    o_ref[...] = jnp.dot(h_scr[...], w_ref[...], preferred_element_type=F32).astype(o_ref.dtype)


def _in_proj(X, modl, g, w, l, d, p, n_rows):
    tm, tn = p.tm_proj, p.tn_proj
    return pl.pallas_call(
        _in_proj_kernel,
        grid=(n_rows // tm, d.IN // tn),
        in_specs=[
            pl.BlockSpec((tm, d.D), lambda i, j: (i, 0)),
            pl.BlockSpec((1, d.D), lambda i, j: (0, 0)),
            _mod_spec(0, tm, d, 2),
            _mod_spec(1, tm, d, 2),
            pl.BlockSpec((None, d.D, tn), lambda i, j: (l, 0, j)),
        ],
        out_specs=pl.BlockSpec((tm, tn), lambda i, j: (i, j)),
        out_shape=jax.ShapeDtypeStruct((n_rows, d.IN), BF16),
        scratch_shapes=[pltpu.VMEM((tm, d.D), BF16)],
        compiler_params=_cparams("arbitrary", "arbitrary"),
        name="in_proj",
    )(X, g, modl, modl, w)


def _diff_scores(q, k, cos, sin):
    qf = q.astype(F32) * Q_SCALE
    if cos is not None:
        qf = _rope(qf, cos, sin)
    first = lax.broadcasted_iota(jnp.int32, qf.shape, 1) < HEAD_DIM
    q1 = jnp.where(first, qf, 0.0).astype(BF16)
    q2 = jnp.where(first, 0.0, qf).astype(BF16)
    return (lax.dot_general(q1, k, _NT_DIMS, preferred_element_type=F32),
            lax.dot_general(q2, k, _NT_DIMS, preferred_element_type=F32))


def _diff_attend(s1, s2, v, lam, g, lam_init):
    e1 = jnp.exp2(s1 - jnp.max(s1, axis=-1, keepdims=True))
    e2 = jnp.exp2(s2 - jnp.max(s2, axis=-1, keepdims=True))
    l1 = jnp.sum(e1, axis=-1, keepdims=True)
    l2 = jnp.sum(e2, axis=-1, keepdims=True)
    a = (e1 - (lam * l1 / l2) * e2).astype(BF16)
    o = jnp.dot(a, v, preferred_element_type=F32) * (1.0 / l1)
    return _rms(o, g) * (1.0 - lam_init)


def _diff_lambda(lam_ref, lam_init):
    lp = lam_ref[...]
    return (jnp.exp(jnp.sum(lp[0:1, :] * lp[1:2, :], axis=-1, keepdims=True))
            - jnp.exp(jnp.sum(lp[2:3, :] * lp[3:4, :], axis=-1, keepdims=True)) + lam_init)


def _attn_a_latent_kernel(q_ref, kc_ref, vc_ref, kl_ref, vl_ref, cq_ref, sq_ref, ck_ref, sk_ref, lam_ref, g_ref,
                          o_ref, k_scr, v_scr, *, lam_init, n_ctx, n_sub):
    @pl.when(pl.program_id(2) == 0)
    def _():
        k_scr[:n_ctx, :] = kc_ref[...]
        k_scr[n_ctx:, :] = _rope(kl_ref[...].astype(F32), ck_ref[...], sk_ref[...]).astype(BF16)
        v_scr[:n_ctx, :] = vc_ref[...]
        v_scr[n_ctx:, :] = vl_ref[...]

    lam = _diff_lambda(lam_ref, lam_init)
    k, v = k_scr[...], v_scr[...]
    rows = q_ref.shape[0] // n_sub
    subs = [slice(sb * rows, (sb + 1) * rows) for sb in range(n_sub)]
    scores = [_diff_scores(q_ref[sl, :], k, cq_ref[sl, :], sq_ref[sl, :]) for sl in subs]
    for sl, (s1, s2) in zip(subs, scores):
        o_ref[sl, :] = _diff_attend(s1, s2, v, lam, g_ref[...], lam_init).astype(o_ref.dtype)


def _attn_a_context_kernel(q_ref, kc_ref, vc_ref, lam_ref, g_ref, o_ref, *, lam_init):
    s1, s2 = _diff_scores(q_ref[...], kc_ref[...], None, None)
    lam = _diff_lambda(lam_ref, lam_init)
    o_ref[...] = _diff_attend(s1, s2, vc_ref[...], lam, g_ref[...], lam_init).astype(o_ref.dtype)


def _attn_a(P, lam_p, subln_g, cos_t, sin_t, lam_init, d, p, latent):
    lane_blk = lambda c: c // LANES
    qb, kb, vb = lane_blk(d.c_Aq), lane_blk(d.c_Ak), lane_blk(d.c_Av)
    ctx_row = d.NL // d.C
    small = [pl.BlockSpec((4, HEAD_DIM), lambda *_: (0, 0)), pl.BlockSpec((1, 2 * HEAD_DIM), lambda *_: (0, 0))]
    if latent:
        tq = p.tq_a
        nq = d.S // tq
        nkeys = d.C + d.S
        q_tab = pl.BlockSpec((tq, LANES), lambda b, h, i: (i, 0))
        k_tab = pl.BlockSpec((d.S, LANES), lambda b, h, i: (0, 0))
        return pl.pallas_call(
            functools.partial(_attn_a_latent_kernel, lam_init=lam_init, n_ctx=d.C, n_sub=max(tq // 256, 1)),
            grid=(d.B, d.A_H, nq),
            in_specs=[
                pl.BlockSpec((tq, LANES), lambda b, h, i: (b * nq + i, qb + h)),
                pl.BlockSpec((d.C, LANES), lambda b, h, i: (ctx_row + b, kb + h)),
                pl.BlockSpec((d.C, LANES), lambda b, h, i: (ctx_row + b, vb + h)),
                pl.BlockSpec((d.S, LANES), lambda b, h, i: (b, kb + h)),
                pl.BlockSpec((d.S, LANES), lambda b, h, i: (b, vb + h)),
                q_tab, q_tab, k_tab, k_tab,
            ] + small,
            out_specs=pl.BlockSpec((tq, LANES), lambda b, h, i: (b * nq + i, h)),
            out_shape=jax.ShapeDtypeStruct((d.NL, d.A_W), BF16),
            scratch_shapes=[pltpu.VMEM((nkeys, LANES), BF16), pltpu.VMEM((nkeys, LANES), BF16)],
            compiler_params=_cparams("arbitrary", "arbitrary", "arbitrary"),
            name="attn_a_latent",
        )(P, P, P, P, P, cos_t, sin_t, cos_t, sin_t, lam_p, subln_g)
    return pl.pallas_call(
        functools.partial(_attn_a_context_kernel, lam_init=lam_init),
        grid=(d.B, d.A_H),
        in_specs=[
            pl.BlockSpec((d.C, LANES), lambda b, h: (ctx_row + b, qb + h)),
            pl.BlockSpec((d.C, LANES), lambda b, h: (ctx_row + b, kb + h)),
            pl.BlockSpec((d.C, LANES), lambda b, h: (ctx_row + b, vb + h)),
        ] + small,
        out_specs=pl.BlockSpec((d.C, LANES), lambda b, h: (b, h)),
        out_shape=jax.ShapeDtypeStruct((d.NC, d.A_W), BF16),
        compiler_params=_cparams("arbitrary", "arbitrary"),
        name="attn_a_context",
    )(P, P, P, lam_p, subln_g)


def _attn_b_kernel(*refs, has_win, G, S, n_sub):
    if has_win:
        sink_ref, q_ref, kc_ref, vc_ref, kl_ref, vl_ref, cq_ref, sq_ref, ck_ref, sk_ref, o_ref, k_scr = refs

        @pl.when(pl.program_id(2) == 0)
        def _():
            k_scr[...] = _rope(kl_ref[...].astype(F32), ck_ref[...], sk_ref[...]).astype(BF16)
    else:
        sink_ref, q_ref, kc_ref, vc_ref, o_ref = refs
    pair = pl.program_id(1)
    nh = 2 * G
    tq = q_ref.shape[0] // n_sub
    low = lax.broadcasted_iota(jnp.int32, (tq, LANES), 1) < HEAD_DIM
    kc, vc = kc_ref[...], vc_ref[...]

    span = BLOCK + 2 * WINDOW
    subs = [slice(sb * tq, (sb + 1) * tq) for sb in range(n_sub)]
    staged = []
    for sb, sl in enumerate(subs):
        qs = []
        for c in range(G):
            qc = q_ref[sl, c * LANES:(c + 1) * LANES].astype(F32) * Q_SCALE
            if has_win:
                qc = _rope(qc, cq_ref[sl, :], sq_ref[sl, :])
            qr = pltpu.roll(qc, HEAD_DIM, 1)
            if c < G // 2:
                qs += [jnp.where(low, qc, 0.0), jnp.where(low, qr, 0.0)]
            else:
                qs += [jnp.where(low, 0.0, qr), jnp.where(low, 0.0, qc)]
        Q = jnp.concatenate(qs, axis=0).astype(BF16)
        s_c = lax.dot_general(Q, kc, _NT_DIMS, preferred_element_type=F32)
        if has_win:
            blk = pl.program_id(2) * n_sub + sb
            start = pl.multiple_of(jnp.clip(blk * BLOCK - WINDOW, 0, S - span), BLOCK)
            s_w = lax.dot_general(Q, k_scr[pl.ds(start, span), :], _NT_DIMS, preferred_element_type=F32)
            staged.append((s_c, s_w, blk, start))
        else:
            staged.append((s_c, None, None, None))

    for sl, (s_c, s_w, blk, start) in zip(subs, staged):
        if has_win:
            qpos = blk * BLOCK + lax.broadcasted_iota(jnp.int32, (tq, span), 0)
            kpos = start + lax.broadcasted_iota(jnp.int32, (tq, span), 1)
            allowed = jnp.abs(qpos - kpos) <= WINDOW
        p_c, p_w, inv = [], [], []
        for hh in range(nh):
            snk = sink_ref[pair * nh + hh] * LOG2E
            sc = s_c[hh * tq:(hh + 1) * tq]
            m = jnp.maximum(jnp.max(sc, axis=-1, keepdims=True), snk)
            if has_win:
                sw = jnp.where(allowed, s_w[hh * tq:(hh + 1) * tq], NEG_INF)
                m = jnp.maximum(m, jnp.max(sw, axis=-1, keepdims=True))
                ew = jnp.exp2(sw - m)
            ec = jnp.exp2(sc - m)
            den = jnp.sum(ec, axis=-1, keepdims=True) + jnp.exp2(snk - m)
            if has_win:
                den = den + jnp.sum(ew, axis=-1, keepdims=True)
                p_w.append(ew.astype(BF16))
            p_c.append(ec.astype(BF16))
            inv.append(1.0 / den)
        o = jnp.dot(jnp.concatenate(p_c, axis=0), vc, preferred_element_type=F32)
        if has_win:
            vw = vl_ref[pl.ds(start, span), :]
            o = o + jnp.dot(jnp.concatenate(p_w, axis=0), vw, preferred_element_type=F32)

        for c in range(G):
            o_lo = o[(2 * c) * tq:(2 * c + 1) * tq] * inv[2 * c]
            o_hi = o[(2 * c + 1) * tq:(2 * c + 2) * tq] * inv[2 * c + 1]
            if c < G // 2:
                chunk = jnp.where(low, o_lo, pltpu.roll(o_hi, HEAD_DIM, 1))
            else:
                chunk = jnp.where(low, pltpu.roll(o_lo, HEAD_DIM, 1), o_hi)
            o_ref[sl, c * LANES:(c + 1) * LANES] = chunk.astype(o_ref.dtype)


def _attn_b(P, sink, cos_t, sin_t, d, p, latent):
    qw = 2 * d.G * HEAD_DIM
    assert d.c_Bq % qw == 0
    n_pairs = d.BKV // LANES
    qb, kb, vb = d.c_Bq // qw, d.c_Bk // LANES, d.c_Bv // LANES
    ctx_row = d.NL // d.C
    sink_spec = pl.BlockSpec(memory_space=pltpu.SMEM)
    if latent:
        tq = p.tq_b
        nq = d.S // tq
        kern = functools.partial(_attn_b_kernel, has_win=True, G=d.G, S=d.S, n_sub=tq // BLOCK)
        q_tab = pl.BlockSpec((tq, LANES), lambda b, j, i: (i, 0))
        k_tab = pl.BlockSpec((d.S, LANES), lambda b, j, i: (0, 0))
        return pl.pallas_call(
            kern,
            grid=(d.B, n_pairs, nq),
            in_specs=[
                sink_spec,
                pl.BlockSpec((tq, qw), lambda b, j, i: (b * nq + i, qb + j)),
                pl.BlockSpec((d.C, LANES), lambda b, j, i: (ctx_row + b, kb + j)),
                pl.BlockSpec((d.C, LANES), lambda b, j, i: (ctx_row + b, vb + j)),
                pl.BlockSpec((d.S, LANES), lambda b, j, i: (b, kb + j)),
                pl.BlockSpec((d.S, LANES), lambda b, j, i: (b, vb + j)),
                q_tab, q_tab, k_tab, k_tab,
            ],
            out_specs=pl.BlockSpec((tq, qw), lambda b, j, i: (b * nq + i, j)),
            out_shape=jax.ShapeDtypeStruct((d.NL, d.B_W), BF16),
            scratch_shapes=[pltpu.VMEM((d.S, LANES), BF16)],
            compiler_params=_cparams("arbitrary", "arbitrary", "arbitrary"),
            name="attn_b_latent",
        )(sink, P, P, P, P, P, cos_t, sin_t, cos_t, sin_t)
    kern = functools.partial(_attn_b_kernel, has_win=False, G=d.G, S=d.S, n_sub=1)
    return pl.pallas_call(
        kern,
        grid=(d.B, n_pairs),
        in_specs=[
            sink_spec,
            pl.BlockSpec((d.C, qw), lambda b, j: (ctx_row + b, qb + j)),
            pl.BlockSpec((d.C, LANES), lambda b, j: (ctx_row + b, kb + j)),
            pl.BlockSpec((d.C, LANES), lambda b, j: (ctx_row + b, vb + j)),
        ],
        out_specs=pl.BlockSpec((d.C, qw), lambda b, j: (b, j)),
        out_shape=jax.ShapeDtypeStruct((d.NC, d.B_W), BF16),
        compiler_params=_cparams("arbitrary", "arbitrary"),
        name="attn_b_context",
    )(sink, P, P, P)


def _merge_kernel(*refs, nl_tiles, has_ctx):
    if has_ctx:
        oa_ref, ob_ref, oac_ref, obc_ref, ga_ref, gb_ref, wa_ref, wb_ref, wo_ref, x_ref, gate_ref, g_ref, o_ref = refs
        is_lat = pl.program_id(0) < nl_tiles
        oa = jnp.where(is_lat, oa_ref[...], oac_ref[...])
        ob = jnp.where(is_lat, ob_ref[...], obc_ref[...])
    else:
        oa_ref, ob_ref, ga_ref, gb_ref, wa_ref, wb_ref, wo_ref, x_ref, gate_ref, g_ref, o_ref = refs
        oa, ob = oa_ref[...], ob_ref[...]
    ya = jnp.dot(oa, wa_ref[...], preferred_element_type=F32)
    yb = jnp.dot(ob, wb_ref[...], preferred_element_type=F32)
    y = jax.nn.sigmoid(ga_ref[...].astype(F32)) * ya + jax.nn.sigmoid(gb_ref[...].astype(F32)) * yb
    z = jnp.dot(y.astype(BF16), wo_ref[...], preferred_element_type=F32)
    o_ref[...] = x_ref[...] + gate_ref[...] * _rms(z, g_ref[...])


def _merge(o_a, o_b, ctx_outs, P, w_branch, w_out, l, X, modl, g, d, p, n_rows):
    tm = p.tm_merge
    once = pl.Buffered(1)
    nl_tiles = d.NL // tm
    has_ctx = ctx_outs is not None
    lat = lambda i: (jnp.minimum(i, nl_tiles - 1), 0)
    ctx = lambda i: (jnp.maximum(i - nl_tiles, 0), 0)
    mixer_specs = [pl.BlockSpec((tm, d.A_W), lat), pl.BlockSpec((tm, d.B_W), lat)]
    if has_ctx:
        mixer_specs += [pl.BlockSpec((tm, d.A_W), ctx), pl.BlockSpec((tm, d.B_W), ctx)]
    return pl.pallas_call(
        functools.partial(_merge_kernel, nl_tiles=nl_tiles, has_ctx=has_ctx),
        grid=(n_rows // tm,),
        in_specs=mixer_specs + [
            pl.BlockSpec((tm, d.D), lambda i: (i, d.c_gA // d.D)),
            pl.BlockSpec((tm, d.D), lambda i: (i, d.c_gB // d.D)),
            pl.BlockSpec((None, None, d.A_W, d.D), lambda i: (l, 0, 0, 0), pipeline_mode=once),
            pl.BlockSpec((None, None, d.B_W, d.D), lambda i: (l, 1, 0, 0), pipeline_mode=once),
            pl.BlockSpec((None, d.D, d.D), lambda i: (l, 0, 0), pipeline_mode=once),
            pl.BlockSpec((tm, d.D), lambda i: (i, 0)),
            _mod_spec(2, tm, d, 1),
            pl.BlockSpec((1, d.D), lambda i: (0, 0)),
        ],
        out_specs=pl.BlockSpec((tm, d.D), lambda i: (i, 0)),
        out_shape=jax.ShapeDtypeStruct((n_rows, d.D), F32),
        compiler_params=_cparams("arbitrary"),
        name="merge",
    )(o_a, o_b, *(ctx_outs or ()), P, P, w_branch, w_branch, w_out, X, modl, g)


def _prenorm_kernel(x_ref, g_ref, shift_ref, scale_ref, o_ref):
    h = _rms(x_ref[...], g_ref[...]) * (1.0 + scale_ref[...]) + shift_ref[...]
    o_ref[...] = h.astype(o_ref.dtype)


def _prenorm(X, modl, g, d, p, n_rows):
    tm = p.tm_row
    return pl.pallas_call(
        _prenorm_kernel,
        grid=(n_rows // tm,),
        in_specs=[
            pl.BlockSpec((tm, d.D), lambda i: (i, 0)),
            pl.BlockSpec((1, d.D), lambda i: (0, 0)),
            _mod_spec(3, tm, d, 1),
            _mod_spec(4, tm, d, 1),
        ],
        out_specs=pl.BlockSpec((tm, d.D), lambda i: (i, 0)),
        out_shape=jax.ShapeDtypeStruct((n_rows, d.D), BF16),
        compiler_params=_cparams("arbitrary"),
        name="prenorm",
    )(X, g, modl, modl)


def _ffn_kernel(te_ref, nv_ref, tr_ref, h_ref, wg_ref, wu_ref, wo_ref, o_ref, h_scr, acc, *, packed):
    i, j = pl.program_id(0), pl.program_id(1)
    rows = tr_ref[i]
    quarter = acc.shape[0] // 4

    @pl.when((rows > 0) & (j == 0))
    def _():
        h_scr[...] = _unpack_halves(h_ref[...]).astype(BF16) if packed else h_ref[...]
        acc[...] = jnp.zeros_like(acc)

    def accumulate(sl):
        h = h_scr[sl, :]
        gt = jnp.dot(h, wg_ref[...], preferred_element_type=F32)
        up = jnp.dot(h, wu_ref[...], preferred_element_type=F32)
        a = (gt * jax.nn.sigmoid(gt) * up).astype(BF16)
        acc[sl, :] += jnp.dot(a, wo_ref[...], preferred_element_type=F32)

    @pl.when(rows > quarter)
    def _():
        accumulate(slice(None))

    @pl.when((rows > 0) & (rows <= quarter))
    def _():
        accumulate(slice(0, quarter))

    @pl.when(j == pl.num_programs(1) - 1)
    def _():
        res = jnp.where(rows > 0, acc[...], 0.0)
        o_ref[...] = _pack_halves(res) if packed else res.astype(o_ref.dtype)


def _ffn(hs, w_in, w_out, tile_expert, n_valid, tile_rows, d, tm, tf, packed):
    R = hs.shape[0]
    F = w_out.shape[1]
    nj = F // tf
    hw = hs.shape[1]

    def row(i, nv):
        return jnp.minimum(i, nv[0] - 1)

    def col(i, j, nv):
        return jnp.where(i < nv[0], j, nj - 1)

    grid_spec = pltpu.PrefetchScalarGridSpec(
        num_scalar_prefetch=3,
        grid=(R // tm, nj),
        in_specs=[
            pl.BlockSpec((tm, hw), lambda i, j, te, nv, tr: (row(i, nv), 0)),
            pl.BlockSpec((None, d.D, tf), lambda i, j, te, nv, tr: (te[row(i, nv)], 0, col(i, j, nv))),
            pl.BlockSpec((None, d.D, tf), lambda i, j, te, nv, tr: (te[row(i, nv)], 0, nj + col(i, j, nv))),
            pl.BlockSpec((None, tf, d.D), lambda i, j, te, nv, tr: (te[row(i, nv)], col(i, j, nv), 0)),
        ],
        out_specs=pl.BlockSpec((tm, hw), lambda i, j, te, nv, tr: (i, 0)),
        scratch_shapes=[pltpu.VMEM((tm, d.D), BF16), pltpu.VMEM((tm, d.D), F32)],
    )
    return pl.pallas_call(
        functools.partial(_ffn_kernel, packed=packed),
        grid_spec=grid_spec,
        out_shape=jax.ShapeDtypeStruct((R, hw), hs.dtype),
        compiler_params=_cparams("arbitrary", "arbitrary"),
        name="ffn_packed" if packed else "ffn_dense",
    )(tile_expert, n_valid, tile_rows, hs, w_in, w_in, w_out)


def _ffn_out_kernel(f_ref, x_ref, gate_ref, g_ref, o_ref):
    o_ref[...] = x_ref[...] + gate_ref[...] * _rms(f_ref[...].astype(F32), g_ref[...])


def _ffn_out(f, X, modl, g, d, p, n_rows):
    tm = p.tm_row
    return pl.pallas_call(
        _ffn_out_kernel,
        grid=(n_rows // tm,),
        in_specs=[
            pl.BlockSpec((tm, d.D), lambda i: (i, 0)),
            pl.BlockSpec((tm, d.D), lambda i: (i, 0)),
            _mod_spec(5, tm, d, 1),
            pl.BlockSpec((1, d.D), lambda i: (0, 0)),
        ],
        out_specs=pl.BlockSpec((tm, d.D), lambda i: (i, 0)),
        out_shape=jax.ShapeDtypeStruct((n_rows, d.D), F32),
        compiler_params=_cparams("arbitrary"),
        name="ffn_out",
    )(f, X, modl, g)


def _router_kernel(x_ref, g_ref, shift_ref, scale_ref, wr_ref, hp_ref, info_ref, wts_ref, cnt_ref, base):
    i = pl.program_id(0)

    @pl.when(i == 0)
    def _():
        base[...] = jnp.zeros_like(base)

    h = _rms(x_ref[...], g_ref[...]) * (1.0 + scale_ref[...]) + shift_ref[...]
    hp_ref[...] = _pack_halves(h)
    logits = jnp.dot(h, wr_ref[...], preferred_element_type=F32)
    tm, E = logits.shape
    lane = lax.broadcasted_iota(jnp.int32, (tm, E), 1).astype(F32)
    m1 = jnp.max(logits, axis=-1, keepdims=True)
    i1 = jnp.min(jnp.where(logits == m1, lane, float(E)), axis=-1, keepdims=True)
    rest = jnp.where(lane == i1, -jnp.inf, logits)
    m2 = jnp.max(rest, axis=-1, keepdims=True)
    i2 = jnp.min(jnp.where(rest == m2, lane, float(E)), axis=-1, keepdims=True)
    e2 = jnp.exp(m2 - m1)
    w1 = 1.0 / (1.0 + e2)
    w2 = e2 * w1

    oh1 = (lane == i1).astype(F32)
    oh2 = (lane == i2).astype(F32)
    oh = oh1 + oh2
    r_i = lax.broadcasted_iota(jnp.int32, (tm, tm), 0)
    c_i = lax.broadcasted_iota(jnp.int32, (tm, tm), 1)
    earlier = (c_i < r_i).astype(BF16)
    before = base[...] + jnp.dot(earlier, oh.astype(BF16), preferred_element_type=F32)
    rank1 = jnp.sum(oh1 * before, axis=-1, keepdims=True)
    rank2 = jnp.sum(oh2 * before, axis=-1, keepdims=True)
    new_base = base[...] + jnp.sum(oh, axis=0, keepdims=True)
    base[...] = new_base
    cnt_ref[...] = new_base.astype(jnp.int32)

    col = lax.broadcasted_iota(jnp.int32, (tm, E), 1)
    info = jnp.where(col == 0, i1, jnp.where(col == 1, i2, jnp.where(col == 2, rank1, jnp.where(col == 3, rank2, 0.0))))
    info_ref[...] = info.astype(jnp.int32)
    wts_ref[...] = jnp.where(col == 0, w1, jnp.where(col == 1, w2, 0.0))


def _router(X, modl, g, w_router, d, p, n_rows):
    tm = p.tm_row
    E = d.E
    return pl.pallas_call(
        _router_kernel,
        grid=(n_rows // tm,),
        in_specs=[
            pl.BlockSpec((tm, d.D), lambda i: (i, 0)),
            pl.BlockSpec((1, d.D), lambda i: (0, 0)),
            _mod_spec(3, tm, d, 1),
            _mod_spec(4, tm, d, 1),
            pl.BlockSpec((d.D, E), lambda i: (0, 0)),
        ],
        out_specs=[
            pl.BlockSpec((tm, d.D // 2), lambda i: (i, 0)),
            pl.BlockSpec((tm, E), lambda i: (i, 0)),
            pl.BlockSpec((tm, E), lambda i: (i, 0)),
            pl.BlockSpec((1, E), lambda i: (0, 0)),
        ],
        out_shape=[
            jax.ShapeDtypeStruct((n_rows, d.D // 2), jnp.int32),
            jax.ShapeDtypeStruct((n_rows, E), jnp.int32),
            jax.ShapeDtypeStruct((n_rows, E), F32),
            jax.ShapeDtypeStruct((1, E), jnp.int32),
        ],
        scratch_shapes=[pltpu.VMEM((1, E), F32)],
        compiler_params=_cparams("arbitrary"),
        name="router",
    )(X, g, modl, modl, w_router)


def _row_copy(src, dst, s, t, sem):
    return pltpu.make_async_copy(src.at[pl.ds(s, 1)], dst.at[pl.ds(t, 1)], sem)


def _dispatch_kernel(d1_ref, d2_ref, h_ref, init_ref, o_ref, sem, *, ch):
    del init_ref

    def issue(k, carry):
        _row_copy(h_ref, o_ref, k, d1_ref[0, k], sem.at[0]).start()
        _row_copy(h_ref, o_ref, k, d2_ref[0, k], sem.at[0]).start()
        return carry

    def drain(k, carry):
        _row_copy(h_ref, o_ref, 0, 0, sem.at[0]).wait()
        _row_copy(h_ref, o_ref, 0, 0, sem.at[0]).wait()
        return carry

    lax.fori_loop(0, ch, issue, 0, unroll=8)
    lax.fori_loop(0, ch, drain, 0, unroll=8)


def _dispatch(h_pk, d1, d2, n_slots, d, p):
    T, hw = h_pk.shape
    ch = p.ch_dma
    idx_spec = pl.BlockSpec((None, 1, ch), lambda i: (i, 0, 0), memory_space=pltpu.SMEM)
    any_spec = pl.BlockSpec(memory_space=pl.ANY)
    return pl.pallas_call(
        functools.partial(_dispatch_kernel, ch=ch),
        grid=(T // ch,),
        in_specs=[idx_spec, idx_spec, pl.BlockSpec((ch, hw), lambda i: (i, 0)), any_spec],
        out_specs=any_spec,
        out_shape=jax.ShapeDtypeStruct((n_slots, hw), jnp.int32),
        scratch_shapes=[pltpu.SemaphoreType.DMA((1,))],
        input_output_aliases={3: 0},
        compiler_params=_cparams("arbitrary"),
        name="moe_dispatch",
    )(d1.reshape(T // ch, 1, ch), d2.reshape(T // ch, 1, ch), h_pk, jnp.zeros((n_slots, hw), jnp.int32))


def _combine_kernel(d1_ref, d2_ref, n1_ref, n2_ref, ys_ref, wts_ref, x_ref, gate_ref, g_ref, o_ref,
                    buf_a, buf_b, sem, *, ch):
    i = pl.program_id(0)
    slot = i % 2

    def gather(i1_ref, i2_ref, s):
        def issue(k, carry):
            _row_copy(ys_ref, buf_a.at[s], i1_ref[0, k], k, sem.at[s]).start()
            _row_copy(ys_ref, buf_b.at[s], i2_ref[0, k], k, sem.at[s]).start()
            return carry
        lax.fori_loop(0, ch, issue, 0, unroll=8)

    @pl.when(i == 0)
    def _():
        gather(d1_ref, d2_ref, 0)

    @pl.when(i + 1 < pl.num_programs(0))
    def _():
        gather(n1_ref, n2_ref, 1 - slot)

    def drain(k, carry):
        _row_copy(ys_ref, buf_a.at[slot], 0, 0, sem.at[slot]).wait()
        _row_copy(ys_ref, buf_b.at[slot], 0, 0, sem.at[slot]).wait()
        return carry

    lax.fori_loop(0, ch, drain, 0, unroll=8)
    w = wts_ref[...]
    f = w[:, 0:1] * _unpack_halves(buf_a[slot]) + w[:, 1:2] * _unpack_halves(buf_b[slot])
    o_ref[...] = x_ref[...] + gate_ref[...] * _rms(f, g_ref[...])


def _combine(ys, d1, d2, wts, X, modl, g, d, p, n_rows):
    ch = p.ch_dma
    hw = ys.shape[1]
    n = n_rows // ch
    idx_spec = pl.BlockSpec((None, 1, ch), lambda i: (i, 0, 0), memory_space=pltpu.SMEM)
    nxt_spec = pl.BlockSpec((None, 1, ch), lambda i: (jnp.minimum(i + 1, n - 1), 0, 0), memory_space=pltpu.SMEM)
    d1, d2 = d1.reshape(n, 1, ch), d2.reshape(n, 1, ch)
    return pl.pallas_call(
        functools.partial(_combine_kernel, ch=ch),
        grid=(n,),
        in_specs=[
            idx_spec, idx_spec, nxt_spec, nxt_spec,
            pl.BlockSpec(memory_space=pl.ANY),
            pl.BlockSpec((ch, d.E), lambda i: (i, 0)),
            pl.BlockSpec((ch, d.D), lambda i: (i, 0)),
            _mod_spec(5, ch, d, 1),
            pl.BlockSpec((1, d.D), lambda i: (0, 0)),
        ],
        out_specs=pl.BlockSpec((ch, d.D), lambda i: (i, 0)),
        out_shape=jax.ShapeDtypeStruct((n_rows, d.D), F32),
        scratch_shapes=[pltpu.VMEM((2, ch, hw), jnp.int32), pltpu.VMEM((2, ch, hw), jnp.int32),
                        pltpu.SemaphoreType.DMA((2,))],
        compiler_params=_cparams("arbitrary"),
        name="moe_combine",
    )(d1, d2, d1, d2, ys, wts, X, modl, g)


def _moe(X, modl, g_pre, g_post, w_router, w_in, w_out, first_expert, d, p, n_rows):
    tm = p.tm_ffn
    h_pk, info, wts, counts = _router(X, modl, g_pre, w_router, d, p, n_rows)
    counts = counts[0]
    padded = (counts + tm - 1) // tm * tm
    ends = jnp.cumsum(padded)
    offs = ends - padded
    d1 = offs[info[:, 0]] + info[:, 2]
    d2 = offs[info[:, 1]] + info[:, 3]
    n_slots = 2 * n_rows + d.E * tm
    tile_end = ends // tm
    tiles = jnp.arange(n_slots // tm, dtype=jnp.int32)
    tile_expert = jnp.minimum(jnp.sum((tiles[:, None] >= tile_end[None, :]).astype(jnp.int32), axis=1), d.E - 1)
    n_valid = tile_end[-1:].astype(jnp.int32)
    tile_rows = jnp.clip(counts[tile_expert] - (tiles - offs[tile_expert] // tm) * tm, 0, tm)
    tile_rows = jnp.where(tiles < n_valid[0], tile_rows, 0).astype(jnp.int32)
    hs = _dispatch(h_pk, d1, d2, n_slots, d, p)
    ys = _ffn(hs, w_in, w_out, tile_expert + first_expert, n_valid, tile_rows, d, tm, p.tf_exp, packed=True)
    return _combine(ys, d1, d2, wts, X, modl, g_post, d, p, n_rows)


def _dense_ffn(X, modl, g_pre, g_post, w_in, w_out, which, d, p, n_rows):
    tm = p.tm_ffn
    h = _prenorm(X, modl, g_pre, d, p, n_rows)
    n_tiles = n_rows // tm
    f = _ffn(h, w_in, w_out, jnp.full((n_tiles,), which, jnp.int32), jnp.full((1,), n_tiles, jnp.int32),
             jnp.full((n_tiles,), tm, jnp.int32), d, tm, p.tf_ffn, packed=False)
    return _ffn_out(f, X, modl, g_post, d, p, n_rows)


def _rope_tables(d):
    rows = d.S // GRID_W
    t_row = jnp.repeat(jnp.arange(rows, dtype=F32), GRID_W)
    t_col = jnp.tile(jnp.arange(GRID_W, dtype=F32), rows)
    half = HEAD_DIM // 2
    inv = ROPE_THETA ** (-jnp.arange(0, half, 2, dtype=F32) / half)
    ang = jnp.concatenate([t_row[:, None] * inv] * 2 + [t_col[:, None] * inv] * 2, axis=1)
    sign = jnp.tile(jnp.concatenate([-jnp.ones(half // 2, F32), jnp.ones(half // 2, F32)]), 2)
    cos = jnp.tile(jnp.cos(ang), (1, LANES // HEAD_DIM))
    sin = jnp.tile(jnp.sin(ang) * sign, (1, LANES // HEAD_DIM))
    return cos, sin


def _permute_in_cols(w, d):
    a_k, a_v, b_k, b_v, a_q, b_q, g_a, g_b = jnp.split(
        w, [d.A_W, 2 * d.A_W, 2 * d.A_W + d.BKV, 2 * d.A_W + 2 * d.BKV, 3 * d.A_W + 2 * d.BKV,
            3 * d.A_W + 2 * d.BKV + d.B_W, 3 * d.A_W + 2 * d.BKV + d.B_W + d.D], axis=-1)
    return jnp.concatenate([g_a, g_b, a_q, b_q, a_k, b_k, b_v, a_v], axis=-1)


def kernel(x, c, ctx, c_ctx, w_ada, b_ada, norm_g, w_in, diff_lambda, diff_subln_g, sink_logit, w_branch, w_out,
           w_ffn_in, w_ffn_out, w_router, w_exp_in, w_exp_out):
    d = _make_dims(x, ctx, w_ada, w_in, sink_logit, w_branch, w_ffn_out, w_router, w_exp_out)
    p = _make_plan(d)

    X = jnp.concatenate([x.reshape(d.NL, d.D), ctx.reshape(d.NC, d.D)], axis=0)
    cond = jnp.concatenate([c, c_ctx[None, :], jnp.zeros((d.RB - d.B - 1, d.D), F32)], axis=0)
    mod_all = _ada_table(cond, w_ada, b_ada, d, p).reshape(d.depth, d.RB * N_MOD, 1, d.D)
    cos_t, sin_t = _rope_tables(d)

    w_in_b = _permute_in_cols(w_in, d).astype(BF16)
    w_branch_b = w_branch.astype(BF16)
    w_out_b = w_out.astype(BF16)
    w_ffn_in_b = w_ffn_in.astype(BF16)
    w_ffn_out_b = w_ffn_out.astype(BF16)
    w_exp_in_b = w_exp_in.astype(BF16).reshape(-1, d.D, 2 * d.FE)
    w_exp_out_b = w_exp_out.astype(BF16).reshape(-1, d.FE, d.D)

    for l in range(d.depth):
        last = l == d.depth - 1
        lam_init = 0.8 - 0.6 * math.exp(-0.3 * l)
        modl = mod_all[l]
        g = norm_g[l][:, None, :]
        n_rows = d.NL if last else d.NT
        subln = diff_subln_g[l][None, :]

        P = _in_proj(X, modl, g[0], w_in_b, l, d, p, d.NT)
        o_a = _attn_a(P, diff_lambda[l], subln, cos_t, sin_t, lam_init, d, p, latent=True)
        o_b = _attn_b(P, sink_logit[l], cos_t, sin_t, d, p, latent=True)
        ctx_outs = None
        if not last:
            ctx_outs = (_attn_a(P, diff_lambda[l], subln, cos_t, sin_t, lam_init, d, p, latent=False),
                        _attn_b(P, sink_logit[l], cos_t, sin_t, d, p, latent=False))
        X = _merge(o_a, o_b, ctx_outs, P, w_branch_b, w_out_b, l, X, modl, g[1], d, p, n_rows)
        if l % 2 == 0:
            X = _dense_ffn(X, modl, g[2], g[3], w_ffn_in_b, w_ffn_out_b, l // 2, d, p, n_rows)
        else:
            X = _moe(X, modl, g[2], g[3], w_router[l // 2], w_exp_in_b, w_exp_out_b, (l // 2) * d.E, d, p, n_rows)
    return X[:d.NL].reshape(d.B, d.S, d.D)
```

```python
import functools
import math
from typing import NamedTuple

import jax
import jax.numpy as jnp
from jax import lax
from jax.experimental import pallas as pl
from jax.experimental.pallas import tpu as pltpu

HEAD_DIM = 64
GRID_W = 64
WINDOW = 128
BLOCK = 128
N_MOD = 6
ROPE_THETA = 10000.0
EPS = 1e-6
NEG_INF = -1e30
LOG2E = math.log2(math.e)
Q_SCALE = HEAD_DIM ** -0.5 * LOG2E
LANES = 128
SUBLANES = 8
VMEM_LIMIT_BYTES = 56 * 2**20
F32 = jnp.float32
BF16 = jnp.bfloat16
_NT_DIMS = (((1,), (1,)), ((), ()))


class Dims(NamedTuple):
    B: int
    S: int
    C: int
    D: int
    depth: int
    A_W: int
    A_H: int
    B_W: int
    B_H: int
    BKV: int
    G: int
    F: int
    E: int
    FE: int
    IN: int
    NL: int
    NC: int
    NT: int
    RB: int
    c_gA: int
    c_gB: int
    c_Aq: int
    c_Bq: int
    c_Ak: int
    c_Bk: int
    c_Bv: int
    c_Av: int


class Plan(NamedTuple):
    tm_proj: int
    tn_proj: int
    tq_a: int
    tq_b: int
    tm_merge: int
    tm_row: int
    tm_ffn: int
    tf_ffn: int
    tf_exp: int
    ch_dma: int
    tn_ada: int


def _pick(pref, *ns):
    t = pref
    while t > 1 and any(n % t for n in ns):
        t //= 2
    return t


def _make_dims(x, ctx, w_ada, w_in, sink_logit, w_branch, w_ffn_out, w_router, w_exp_out):
    B, S, D = x.shape
    C = ctx.shape[1]
    depth = w_ada.shape[0]
    A_W = w_branch.shape[2]
    B_W = A_W
    A_H = A_W // (2 * HEAD_DIM)
    B_H = sink_logit.shape[1]
    IN = w_in.shape[2]
    BKV = (IN - 3 * A_W - B_W - 2 * D) // 2
    G = B_H // (BKV // HEAD_DIM)
    assert B_H * HEAD_DIM == B_W and G % 2 == 0 and (BKV // HEAD_DIM) % 2 == 0
    assert S % GRID_W == 0 and S >= BLOCK + 2 * WINDOW and S % BLOCK == 0
    c_gA, c_gB, c_Aq = 0, D, 2 * D
    c_Bq = c_Aq + A_W
    c_Ak = c_Bq + B_W
    c_Bk = c_Ak + A_W
    c_Bv = c_Bk + BKV
    c_Av = c_Bv + BKV
    assert c_Av + A_W == IN
    RB = -(-(B + 1) // SUBLANES) * SUBLANES
    return Dims(B, S, C, D, depth, A_W, A_H, B_W, B_H, BKV, G, w_ffn_out.shape[1], w_router.shape[2],
                w_exp_out.shape[2], IN, B * S, B * C, B * S + B * C, RB,
                c_gA, c_gB, c_Aq, c_Bq, c_Ak, c_Bk, c_Bv, c_Av)


def _make_plan(d):
    return Plan(
        tm_proj=_pick(1024, d.S, d.NC),
        tn_proj=_pick(512, d.IN),
        tq_a=_pick(4 * 256, d.S),
        tq_b=_pick(2 * BLOCK, d.S),
        tm_merge=_pick(256, d.S, d.NC),
        tm_row=_pick(512, d.S, d.NC),
        tm_ffn=_pick(1024, d.NL, d.NC),
        tf_ffn=_pick(512, d.F),
        tf_exp=_pick(512, d.FE),
        ch_dma=_pick(512, d.S, d.NC),
        tn_ada=_pick(1024, N_MOD * d.D),
    )


def _cparams(*sem):
    return pltpu.CompilerParams(dimension_semantics=sem, vmem_limit_bytes=VMEM_LIMIT_BYTES)


def _rms(x, g):
    return x * lax.rsqrt(jnp.mean(x * x, axis=-1, keepdims=True) + EPS) * g


def _pack_halves(x):
    n = x.shape[1] // 2
    lo = lax.bitcast_convert_type(x[:, :n].astype(BF16).astype(F32), jnp.int32)
    hi = lax.bitcast_convert_type(x[:, n:].astype(BF16).astype(F32), jnp.int32)
    return lax.shift_right_logical(lo, jnp.int32(16)) | (hi & jnp.int32(-65536))


def _unpack_halves(w):
    lo = lax.bitcast_convert_type(lax.shift_left(w, jnp.int32(16)), F32)
    hi = lax.bitcast_convert_type(w & jnp.int32(-65536), F32)
    return jnp.concatenate([lo, hi], axis=1)


def _rope(x, cos, sin):
    lane = lax.broadcasted_iota(jnp.int32, x.shape, 1)
    partner = jnp.where((lane % 32) < 16, pltpu.roll(x, LANES - 16, 1), pltpu.roll(x, 16, 1))
    return x * cos + partner * sin


def _mod_row(i, tm, d):
    return jnp.where(i < d.NL // tm, (i * tm) // d.S, d.B)


def _mod_spec(k, tm, d, nargs):
    if nargs == 1:
        return pl.BlockSpec((None, 1, d.D), lambda i: (_mod_row(i, tm, d) * N_MOD + k, 0, 0))
    return pl.BlockSpec((None, 1, d.D), lambda i, j: (_mod_row(i, tm, d) * N_MOD + k, 0, 0))


def _ada_kernel(s_ref, w_ref, b_ref, o_ref):
    s = s_ref[...]
    s = (s * jax.nn.sigmoid(s)).astype(BF16)
    o_ref[...] = jnp.dot(s, w_ref[...].astype(BF16), preferred_element_type=F32) + b_ref[...]


def _ada_table(cond, w_ada, b_ada, d, p):
    n6 = N_MOD * d.D
    return pl.pallas_call(
        _ada_kernel,
        grid=(d.depth, n6 // p.tn_ada),
        in_specs=[
            pl.BlockSpec((d.RB, d.D), lambda l, j: (0, 0)),
            pl.BlockSpec((None, d.D, p.tn_ada), lambda l, j: (l, 0, j)),
            pl.BlockSpec((None, 1, p.tn_ada), lambda l, j: (l, 0, j)),
        ],
        out_specs=pl.BlockSpec((None, d.RB, p.tn_ada), lambda l, j: (l, 0, j)),
        out_shape=jax.ShapeDtypeStruct((d.depth, d.RB, n6), F32),
        compiler_params=_cparams("arbitrary", "arbitrary"),
        name="ada_table",
    )(cond, w_ada, b_ada.reshape(d.depth, 1, n6))


def _in_proj_kernel(x_ref, g_ref, shift_ref, scale_ref, w_ref, o_ref, h_scr):
    @pl.when(pl.program_id(1) == 0)
    def _():
        h = _rms(x_ref[...], g_ref[...]) * (1.0 + scale_ref[...]) + shift_ref[...]
        h_scr[...] = h.astype(BF16)

    o_ref[...] = jnp.dot(h_scr[...], w_ref[...], preferred_element_type=F32).astype(o_ref.dtype)


def _in_proj(X, modl, g, w, l, d, p, n_rows):
    tm, tn = p.tm_proj, p.tn_proj
    return pl.pallas_call(
        _in_proj_kernel,
        grid=(n_rows // tm, d.IN // tn),
        in_specs=[
            pl.BlockSpec((tm, d.D), lambda i, j: (i, 0)),
            pl.BlockSpec((1, d.D), lambda i, j: (0, 0)),
            _mod_spec(0, tm, d, 2),
            _mod_spec(1, tm, d, 2),
            pl.BlockSpec((None, d.D, tn), lambda i, j: (l, 0, j)),
        ],
        out_specs=pl.BlockSpec((tm, tn), lambda i, j: (i, j)),
        out_shape=jax.ShapeDtypeStruct((n_rows, d.IN), BF16),
        scratch_shapes=[pltpu.VMEM((tm, d.D), BF16)],
        compiler_params=_cparams("arbitrary", "arbitrary"),
        name="in_proj",
    )(X, g, modl, modl, w)


def _diff_scores(q, k, cos, sin):
    qf = q.astype(F32) * Q_SCALE
    if cos is not None:
        qf = _rope(qf, cos, sin)
    first = lax.broadcasted_iota(jnp.int32, qf.shape, 1) < HEAD_DIM
    q1 = jnp.where(first, qf, 0.0).astype(BF16)
    q2 = jnp.where(first, 0.0, qf).astype(BF16)
    return (lax.dot_general(q1, k, _NT_DIMS, preferred_element_type=F32),
            lax.dot_general(q2, k, _NT_DIMS, preferred_element_type=F32))


def _diff_attend(s1, s2, v, lam, g, lam_init):
    e1 = jnp.exp2(s1 - jnp.max(s1, axis=-1, keepdims=True))
    e2 = jnp.exp2(s2 - jnp.max(s2, axis=-1, keepdims=True))
    l1 = jnp.sum(e1, axis=-1, keepdims=True)
    l2 = jnp.sum(e2, axis=-1, keepdims=True)
    a = (e1 - (lam * l1 / l2) * e2).astype(BF16)
    o = jnp.dot(a, v, preferred_element_type=F32) * (1.0 / l1)
    return _rms(o, g) * (1.0 - lam_init)


def _diff_lambda(lam_ref, lam_init):
    lp = lam_ref[...]
    return (jnp.exp(jnp.sum(lp[0:1, :] * lp[1:2, :], axis=-1, keepdims=True))
            - jnp.exp(jnp.sum(lp[2:3, :] * lp[3:4, :], axis=-1, keepdims=True)) + lam_init)


def _attn_a_latent_kernel(q_ref, kc_ref, vc_ref, kl_ref, vl_ref, cq_ref, sq_ref, ck_ref, sk_ref, lam_ref, g_ref,
                          o_ref, k_scr, v_scr, *, lam_init, n_ctx, n_sub):
    @pl.when(pl.program_id(2) == 0)
    def _():
        k_scr[:n_ctx, :] = kc_ref[...]
        k_scr[n_ctx:, :] = _rope(kl_ref[...].astype(F32), ck_ref[...], sk_ref[...]).astype(BF16)
        v_scr[:n_ctx, :] = vc_ref[...]
        v_scr[n_ctx:, :] = vl_ref[...]

    lam = _diff_lambda(lam_ref, lam_init)
    k, v = k_scr[...], v_scr[...]
    rows = q_ref.shape[0] // n_sub
    subs = [slice(sb * rows, (sb + 1) * rows) for sb in range(n_sub)]
    scores = [_diff_scores(q_ref[sl, :], k, cq_ref[sl, :], sq_ref[sl, :]) for sl in subs]
    for sl, (s1, s2) in zip(subs, scores):
        o_ref[sl, :] = _diff_attend(s1, s2, v, lam, g_ref[...], lam_init).astype(o_ref.dtype)


def _attn_a_context_kernel(q_ref, kc_ref, vc_ref, lam_ref, g_ref, o_ref, *, lam_init):
    s1, s2 = _diff_scores(q_ref[...], kc_ref[...], None, None)
    lam = _diff_lambda(lam_ref, lam_init)
    o_ref[...] = _diff_attend(s1, s2, vc_ref[...], lam, g_ref[...], lam_init).astype(o_ref.dtype)


def _attn_a(P, lam_p, subln_g, cos_t, sin_t, lam_init, d, p, latent):
    lane_blk = lambda c: c // LANES
    qb, kb, vb = lane_blk(d.c_Aq), lane_blk(d.c_Ak), lane_blk(d.c_Av)
    ctx_row = d.NL // d.C
    small = [pl.BlockSpec((4, HEAD_DIM), lambda *_: (0, 0)), pl.BlockSpec((1, 2 * HEAD_DIM), lambda *_: (0, 0))]
    if latent:
        tq = p.tq_a
        nq = d.S // tq
        nkeys = d.C + d.S
        q_tab = pl.BlockSpec((tq, LANES), lambda b, h, i: (i, 0))
        k_tab = pl.BlockSpec((d.S, LANES), lambda b, h, i: (0, 0))
        return pl.pallas_call(
            functools.partial(_attn_a_latent_kernel, lam_init=lam_init, n_ctx=d.C, n_sub=max(tq // 256, 1)),
            grid=(d.B, d.A_H, nq),
            in_specs=[
                pl.BlockSpec((tq, LANES), lambda b, h, i: (b * nq + i, qb + h)),
                pl.BlockSpec((d.C, LANES), lambda b, h, i: (ctx_row + b, kb + h)),
                pl.BlockSpec((d.C, LANES), lambda b, h, i: (ctx_row + b, vb + h)),
                pl.BlockSpec((d.S, LANES), lambda b, h, i: (b, kb + h)),
                pl.BlockSpec((d.S, LANES), lambda b, h, i: (b, vb + h)),
                q_tab, q_tab, k_tab, k_tab,
            ] + small,
            out_specs=pl.BlockSpec((tq, LANES), lambda b, h, i: (b * nq + i, h)),
            out_shape=jax.ShapeDtypeStruct((d.NL, d.A_W), BF16),
            scratch_shapes=[pltpu.VMEM((nkeys, LANES), BF16), pltpu.VMEM((nkeys, LANES), BF16)],
            compiler_params=_cparams("arbitrary", "arbitrary", "arbitrary"),
            name="attn_a_latent",
        )(P, P, P, P, P, cos_t, sin_t, cos_t, sin_t, lam_p, subln_g)
    return pl.pallas_call(
        functools.partial(_attn_a_context_kernel, lam_init=lam_init),
        grid=(d.B, d.A_H),
        in_specs=[
            pl.BlockSpec((d.C, LANES), lambda b, h: (ctx_row + b, qb + h)),
            pl.BlockSpec((d.C, LANES), lambda b, h: (ctx_row + b, kb + h)),
            pl.BlockSpec((d.C, LANES), lambda b, h: (ctx_row + b, vb + h)),
        ] + small,
        out_specs=pl.BlockSpec((d.C, LANES), lambda b, h: (b, h)),
        out_shape=jax.ShapeDtypeStruct((d.NC, d.A_W), BF16),
        compiler_params=_cparams("arbitrary", "arbitrary"),
        name="attn_a_context",
    )(P, P, P, lam_p, subln_g)


def _attn_b_kernel(*refs, has_win, G, S, n_sub):
    if has_win:
        sink_ref, q_ref, kc_ref, vc_ref, kl_ref, vl_ref, cq_ref, sq_ref, ck_ref, sk_ref, o_ref, k_scr = refs

        @pl.when(pl.program_id(2) == 0)
        def _():
            k_scr[...] = _rope(kl_ref[...].astype(F32), ck_ref[...], sk_ref[...]).astype(BF16)
    else:
        sink_ref, q_ref, kc_ref, vc_ref, o_ref = refs
    pair = pl.program_id(1)
    nh = 2 * G
    tq = q_ref.shape[0] // n_sub
    low = lax.broadcasted_iota(jnp.int32, (tq, LANES), 1) < HEAD_DIM
    kc, vc = kc_ref[...], vc_ref[...]

    span = BLOCK + 2 * WINDOW
    subs = [slice(sb * tq, (sb + 1) * tq) for sb in range(n_sub)]
    staged = []
    for sb, sl in enumerate(subs):
        qs = []
        for c in range(G):
            qc = q_ref[sl, c * LANES:(c + 1) * LANES].astype(F32) * Q_SCALE
            if has_win:
                qc = _rope(qc, cq_ref[sl, :], sq_ref[sl, :])
            qr = pltpu.roll(qc, HEAD_DIM, 1)
            if c < G // 2:
                qs += [jnp.where(low, qc, 0.0), jnp.where(low, qr, 0.0)]
            else:
                qs += [jnp.where(low, 0.0, qr), jnp.where(low, 0.0, qc)]
        Q = jnp.concatenate(qs, axis=0).astype(BF16)
        s_c = lax.dot_general(Q, kc, _NT_DIMS, preferred_element_type=F32)
        if has_win:
            blk = pl.program_id(2) * n_sub + sb
            start = pl.multiple_of(jnp.clip(blk * BLOCK - WINDOW, 0, S - span), BLOCK)
            s_w = lax.dot_general(Q, k_scr[pl.ds(start, span), :], _NT_DIMS, preferred_element_type=F32)
            staged.append((s_c, s_w, blk, start))
        else:
            staged.append((s_c, None, None, None))

    for sl, (s_c, s_w, blk, start) in zip(subs, staged):
        if has_win:
            qpos = blk * BLOCK + lax.broadcasted_iota(jnp.int32, (tq, span), 0)
            kpos = start + lax.broadcasted_iota(jnp.int32, (tq, span), 1)
            allowed = jnp.abs(qpos - kpos) <= WINDOW
        p_c, p_w, inv = [], [], []
        for hh in range(nh):
            snk = sink_ref[pair * nh + hh] * LOG2E
            sc = s_c[hh * tq:(hh + 1) * tq]
            m = jnp.maximum(jnp.max(sc, axis=-1, keepdims=True), snk)
            if has_win:
                sw = jnp.where(allowed, s_w[hh * tq:(hh + 1) * tq], NEG_INF)
                m = jnp.maximum(m, jnp.max(sw, axis=-1, keepdims=True))
                ew = jnp.exp2(sw - m)
            ec = jnp.exp2(sc - m)
            den = jnp.sum(ec, axis=-1, keepdims=True) + jnp.exp2(snk - m)
            if has_win:
                den = den + jnp.sum(ew, axis=-1, keepdims=True)
                p_w.append(ew.astype(BF16))
            p_c.append(ec.astype(BF16))
            inv.append(1.0 / den)
        o = jnp.dot(jnp.concatenate(p_c, axis=0), vc, preferred_element_type=F32)
        if has_win:
            vw = vl_ref[pl.ds(start, span), :]
            o = o + jnp.dot(jnp.concatenate(p_w, axis=0), vw, preferred_element_type=F32)

        for c in range(G):
            o_lo = o[(2 * c) * tq:(2 * c + 1) * tq] * inv[2 * c]
            o_hi = o[(2 * c + 1) * tq:(2 * c + 2) * tq] * inv[2 * c + 1]
            if c < G // 2:
                chunk = jnp.where(low, o_lo, pltpu.roll(o_hi, HEAD_DIM, 1))
            else:
                chunk = jnp.where(low, pltpu.roll(o_lo, HEAD_DIM, 1), o_hi)
            o_ref[sl, c * LANES:(c + 1) * LANES] = chunk.astype(o_ref.dtype)


def _attn_b(P, sink, cos_t, sin_t, d, p, latent):
    qw = 2 * d.G * HEAD_DIM
    assert d.c_Bq % qw == 0
    n_pairs = d.BKV // LANES
    qb, kb, vb = d.c_Bq // qw, d.c_Bk // LANES, d.c_Bv // LANES
    ctx_row = d.NL // d.C
    sink_spec = pl.BlockSpec(memory_space=pltpu.SMEM)
    if latent:
        tq = p.tq_b
        nq = d.S // tq
        kern = functools.partial(_attn_b_kernel, has_win=True, G=d.G, S=d.S, n_sub=tq // BLOCK)
        q_tab = pl.BlockSpec((tq, LANES), lambda b, j, i: (i, 0))
        k_tab = pl.BlockSpec((d.S, LANES), lambda b, j, i: (0, 0))
        return pl.pallas_call(
            kern,
            grid=(d.B, n_pairs, nq),
            in_specs=[
                sink_spec,
                pl.BlockSpec((tq, qw), lambda b, j, i: (b * nq + i, qb + j)),
                pl.BlockSpec((d.C, LANES), lambda b, j, i: (ctx_row + b, kb + j)),
                pl.BlockSpec((d.C, LANES), lambda b, j, i: (ctx_row + b, vb + j)),
                pl.BlockSpec((d.S, LANES), lambda b, j, i: (b, kb + j)),
                pl.BlockSpec((d.S, LANES), lambda b, j, i: (b, vb + j)),
                q_tab, q_tab, k_tab, k_tab,
            ],
            out_specs=pl.BlockSpec((tq, qw), lambda b, j, i: (b * nq + i, j)),
            out_shape=jax.ShapeDtypeStruct((d.NL, d.B_W), BF16),
            scratch_shapes=[pltpu.VMEM((d.S, LANES), BF16)],
            compiler_params=_cparams("arbitrary", "arbitrary", "arbitrary"),
            name="attn_b_latent",
        )(sink, P, P, P, P, P, cos_t, sin_t, cos_t, sin_t)
    kern = functools.partial(_attn_b_kernel, has_win=False, G=d.G, S=d.S, n_sub=1)
    return pl.pallas_call(
        kern,
        grid=(d.B, n_pairs),
        in_specs=[
            sink_spec,
            pl.BlockSpec((d.C, qw), lambda b, j: (ctx_row + b, qb + j)),
            pl.BlockSpec((d.C, LANES), lambda b, j: (ctx_row + b, kb + j)),
            pl.BlockSpec((d.C, LANES), lambda b, j: (ctx_row + b, vb + j)),
        ],
        out_specs=pl.BlockSpec((d.C, qw), lambda b, j: (b, j)),
        out_shape=jax.ShapeDtypeStruct((d.NC, d.B_W), BF16),
        compiler_params=_cparams("arbitrary", "arbitrary"),
        name="attn_b_context",
    )(sink, P, P, P)


def _merge_kernel(*refs, nl_tiles, has_ctx):
    if has_ctx:
        oa_ref, ob_ref, oac_ref, obc_ref, ga_ref, gb_ref, wa_ref, wb_ref, wo_ref, x_ref, gate_ref, g_ref, o_ref = refs
        is_lat = pl.program_id(0) < nl_tiles
        oa = jnp.where(is_lat, oa_ref[...], oac_ref[...])
        ob = jnp.where(is_lat, ob_ref[...], obc_ref[...])
    else:
        oa_ref, ob_ref, ga_ref, gb_ref, wa_ref, wb_ref, wo_ref, x_ref, gate_ref, g_ref, o_ref = refs
        oa, ob = oa_ref[...], ob_ref[...]
    ya = jnp.dot(oa, wa_ref[...], preferred_element_type=F32)
    yb = jnp.dot(ob, wb_ref[...], preferred_element_type=F32)
    y = jax.nn.sigmoid(ga_ref[...].astype(F32)) * ya + jax.nn.sigmoid(gb_ref[...].astype(F32)) * yb
    z = jnp.dot(y.astype(BF16), wo_ref[...], preferred_element_type=F32)
    o_ref[...] = x_ref[...] + gate_ref[...] * _rms(z, g_ref[...])


def _merge(o_a, o_b, ctx_outs, P, w_branch, w_out, l, X, modl, g, d, p, n_rows):
    tm = p.tm_merge
    once = pl.Buffered(1)
    nl_tiles = d.NL // tm
    has_ctx = ctx_outs is not None
    lat = lambda i: (jnp.minimum(i, nl_tiles - 1), 0)
    ctx = lambda i: (jnp.maximum(i - nl_tiles, 0), 0)
    mixer_specs = [pl.BlockSpec((tm, d.A_W), lat), pl.BlockSpec((tm, d.B_W), lat)]
    if has_ctx:
        mixer_specs += [pl.BlockSpec((tm, d.A_W), ctx), pl.BlockSpec((tm, d.B_W), ctx)]
    return pl.pallas_call(
        functools.partial(_merge_kernel, nl_tiles=nl_tiles, has_ctx=has_ctx),
        grid=(n_rows // tm,),
        in_specs=mixer_specs + [
            pl.BlockSpec((tm, d.D), lambda i: (i, d.c_gA // d.D)),
            pl.BlockSpec((tm, d.D), lambda i: (i, d.c_gB // d.D)),
            pl.BlockSpec((None, None, d.A_W, d.D), lambda i: (l, 0, 0, 0), pipeline_mode=once),
            pl.BlockSpec((None, None, d.B_W, d.D), lambda i: (l, 1, 0, 0), pipeline_mode=once),
            pl.BlockSpec((None, d.D, d.D), lambda i: (l, 0, 0), pipeline_mode=once),
            pl.BlockSpec((tm, d.D), lambda i: (i, 0)),
            _mod_spec(2, tm, d, 1),
            pl.BlockSpec((1, d.D), lambda i: (0, 0)),
        ],
        out_specs=pl.BlockSpec((tm, d.D), lambda i: (i, 0)),
        out_shape=jax.ShapeDtypeStruct((n_rows, d.D), F32),
        compiler_params=_cparams("arbitrary"),
        name="merge",
    )(o_a, o_b, *(ctx_outs or ()), P, P, w_branch, w_branch, w_out, X, modl, g)


def _prenorm_kernel(x_ref, g_ref, shift_ref, scale_ref, o_ref):
    h = _rms(x_ref[...], g_ref[...]) * (1.0 + scale_ref[...]) + shift_ref[...]
    o_ref[...] = h.astype(o_ref.dtype)


def _prenorm(X, modl, g, d, p, n_rows):
    tm = p.tm_row
    return pl.pallas_call(
        _prenorm_kernel,
        grid=(n_rows // tm,),
        in_specs=[
            pl.BlockSpec((tm, d.D), lambda i: (i, 0)),
            pl.BlockSpec((1, d.D), lambda i: (0, 0)),
            _mod_spec(3, tm, d, 1),
            _mod_spec(4, tm, d, 1),
        ],
        out_specs=pl.BlockSpec((tm, d.D), lambda i: (i, 0)),
        out_shape=jax.ShapeDtypeStruct((n_rows, d.D), BF16),
        compiler_params=_cparams("arbitrary"),
        name="prenorm",
    )(X, g, modl, modl)


def _ffn_kernel(te_ref, nv_ref, tr_ref, h_ref, wg_ref, wu_ref, wo_ref, o_ref, h_scr, acc, *, packed):
    i, j = pl.program_id(0), pl.program_id(1)
    rows = tr_ref[i]
    n_parts = 4
    part = acc.shape[0] // n_parts

    @pl.when((rows > 0) & (j == 0))
    def _():
        h_scr[...] = _unpack_halves(h_ref[...]).astype(BF16) if packed else h_ref[...]
        acc[...] = jnp.zeros_like(acc)

    def accumulate(sl):
        h = h_scr[sl, :]
        gt = jnp.dot(h, wg_ref[...], preferred_element_type=F32)
        up = jnp.dot(h, wu_ref[...], preferred_element_type=F32)
        a = (gt * jax.nn.sigmoid(gt) * up).astype(BF16)
        acc[sl, :] += jnp.dot(a, wo_ref[...], preferred_element_type=F32)

    for q in range(1, n_parts + 1):
        @pl.when((rows > (q - 1) * part) & (rows <= q * part))
        def _(q=q):
            accumulate(slice(0, q * part))

    @pl.when(j == pl.num_programs(1) - 1)
    def _():
        res = jnp.where(rows > 0, acc[...], 0.0)
        o_ref[...] = _pack_halves(res) if packed else res.astype(o_ref.dtype)


def _ffn(hs, w_in, w_out, tile_expert, n_valid, tile_rows, d, tm, tf, packed):
    R = hs.shape[0]
    F = w_out.shape[1]
    nj = F // tf
    hw = hs.shape[1]

    def row(i, nv):
        return jnp.minimum(i, nv[0] - 1)

    def col(i, j, nv):
        return jnp.where(i < nv[0], j, nj - 1)

    grid_spec = pltpu.PrefetchScalarGridSpec(
        num_scalar_prefetch=3,
        grid=(R // tm, nj),
        in_specs=[
            pl.BlockSpec((tm, hw), lambda i, j, te, nv, tr: (row(i, nv), 0)),
            pl.BlockSpec((None, d.D, tf), lambda i, j, te, nv, tr: (te[row(i, nv)], 0, col(i, j, nv))),
            pl.BlockSpec((None, d.D, tf), lambda i, j, te, nv, tr: (te[row(i, nv)], 0, nj + col(i, j, nv))),
            pl.BlockSpec((None, tf, d.D), lambda i, j, te, nv, tr: (te[row(i, nv)], col(i, j, nv), 0)),
        ],
        out_specs=pl.BlockSpec((tm, hw), lambda i, j, te, nv, tr: (i, 0)),
        scratch_shapes=[pltpu.VMEM((tm, d.D), BF16), pltpu.VMEM((tm, d.D), F32)],
    )
    return pl.pallas_call(
        functools.partial(_ffn_kernel, packed=packed),
        grid_spec=grid_spec,
        out_shape=jax.ShapeDtypeStruct((R, hw), hs.dtype),
        compiler_params=_cparams("arbitrary", "arbitrary"),
        name="ffn_packed" if packed else "ffn_dense",
    )(tile_expert, n_valid, tile_rows, hs, w_in, w_in, w_out)


def _ffn_out_kernel(f_ref, x_ref, gate_ref, g_ref, o_ref):
    o_ref[...] = x_ref[...] + gate_ref[...] * _rms(f_ref[...].astype(F32), g_ref[...])


def _ffn_out(f, X, modl, g, d, p, n_rows):
    tm = p.tm_row
    return pl.pallas_call(
        _ffn_out_kernel,
        grid=(n_rows // tm,),
        in_specs=[
            pl.BlockSpec((tm, d.D), lambda i: (i, 0)),
            pl.BlockSpec((tm, d.D), lambda i: (i, 0)),
            _mod_spec(5, tm, d, 1),
            pl.BlockSpec((1, d.D), lambda i: (0, 0)),
        ],
        out_specs=pl.BlockSpec((tm, d.D), lambda i: (i, 0)),
        out_shape=jax.ShapeDtypeStruct((n_rows, d.D), F32),
        compiler_params=_cparams("arbitrary"),
        name="ffn_out",
    )(f, X, modl, g)


def _router_kernel(x_ref, g_ref, shift_ref, scale_ref, wr_ref, hp_ref, info_ref, wts_ref, cnt_ref, base):
    i = pl.program_id(0)

    @pl.when(i == 0)
    def _():
        base[...] = jnp.zeros_like(base)

    h = _rms(x_ref[...], g_ref[...]) * (1.0 + scale_ref[...]) + shift_ref[...]
    hp_ref[...] = _pack_halves(h)
    logits = jnp.dot(h, wr_ref[...], preferred_element_type=F32)
    tm, E = logits.shape
    lane = lax.broadcasted_iota(jnp.int32, (tm, E), 1).astype(F32)
    m1 = jnp.max(logits, axis=-1, keepdims=True)
    i1 = jnp.min(jnp.where(logits == m1, lane, float(E)), axis=-1, keepdims=True)
    rest = jnp.where(lane == i1, -jnp.inf, logits)
    m2 = jnp.max(rest, axis=-1, keepdims=True)
    i2 = jnp.min(jnp.where(rest == m2, lane, float(E)), axis=-1, keepdims=True)
    e2 = jnp.exp(m2 - m1)
    w1 = 1.0 / (1.0 + e2)
    w2 = e2 * w1

    oh1 = (lane == i1).astype(F32)
    oh2 = (lane == i2).astype(F32)
    oh = oh1 + oh2
    r_i = lax.broadcasted_iota(jnp.int32, (tm, tm), 0)
    c_i = lax.broadcasted_iota(jnp.int32, (tm, tm), 1)
    earlier = (c_i < r_i).astype(BF16)
    before = base[...] + jnp.dot(earlier, oh.astype(BF16), preferred_element_type=F32)
    rank1 = jnp.sum(oh1 * before, axis=-1, keepdims=True)
    rank2 = jnp.sum(oh2 * before, axis=-1, keepdims=True)
    new_base = base[...] + jnp.sum(oh, axis=0, keepdims=True)
    base[...] = new_base
    cnt_ref[...] = new_base.astype(jnp.int32)

    col = lax.broadcasted_iota(jnp.int32, (tm, E), 1)
    info = jnp.where(col == 0, i1, jnp.where(col == 1, i2, jnp.where(col == 2, rank1, jnp.where(col == 3, rank2, 0.0))))
    info_ref[...] = info.astype(jnp.int32)
    wts_ref[...] = jnp.where(col == 0, w1, jnp.where(col == 1, w2, 0.0))


def _router(X, modl, g, w_router, d, p, n_rows):
    tm = p.tm_row
    E = d.E
    return pl.pallas_call(
        _router_kernel,
        grid=(n_rows // tm,),
        in_specs=[
            pl.BlockSpec((tm, d.D), lambda i: (i, 0)),
            pl.BlockSpec((1, d.D), lambda i: (0, 0)),
            _mod_spec(3, tm, d, 1),
            _mod_spec(4, tm, d, 1),
            pl.BlockSpec((d.D, E), lambda i: (0, 0)),
        ],
        out_specs=[
            pl.BlockSpec((tm, d.D // 2), lambda i: (i, 0)),
            pl.BlockSpec((tm, E), lambda i: (i, 0)),
            pl.BlockSpec((tm, E), lambda i: (i, 0)),
            pl.BlockSpec((1, E), lambda i: (0, 0)),
        ],
        out_shape=[
            jax.ShapeDtypeStruct((n_rows, d.D // 2), jnp.int32),
            jax.ShapeDtypeStruct((n_rows, E), jnp.int32),
            jax.ShapeDtypeStruct((n_rows, E), F32),
            jax.ShapeDtypeStruct((1, E), jnp.int32),
        ],
        scratch_shapes=[pltpu.VMEM((1, E), F32)],
        compiler_params=_cparams("arbitrary"),
        name="router",
    )(X, g, modl, modl, w_router)


def _row_copy(src, dst, s, t, sem):
    return pltpu.make_async_copy(src.at[pl.ds(s, 1)], dst.at[pl.ds(t, 1)], sem)


def _dispatch_kernel(d1_ref, d2_ref, h_ref, init_ref, o_ref, sem, *, ch):
    del init_ref

    def issue(k, carry):
        _row_copy(h_ref, o_ref, k, d1_ref[0, k], sem.at[0]).start()
        _row_copy(h_ref, o_ref, k, d2_ref[0, k], sem.at[0]).start()
        return carry

    lax.fori_loop(0, ch, issue, 0, unroll=8)
    for _ in range(2):
        pltpu.make_async_copy(h_ref, o_ref.at[pl.ds(0, ch)], sem.at[0]).wait()


def _dispatch(h_pk, d1, d2, n_slots, d, p):
    T, hw = h_pk.shape
    ch = p.ch_dma
    idx_spec = pl.BlockSpec((None, 1, ch), lambda i: (i, 0, 0), memory_space=pltpu.SMEM)
    any_spec = pl.BlockSpec(memory_space=pl.ANY)
    return pl.pallas_call(
        functools.partial(_dispatch_kernel, ch=ch),
        grid=(T // ch,),
        in_specs=[idx_spec, idx_spec, pl.BlockSpec((ch, hw), lambda i: (i, 0)), any_spec],
        out_specs=any_spec,
        out_shape=jax.ShapeDtypeStruct((n_slots, hw), jnp.int32),
        scratch_shapes=[pltpu.SemaphoreType.DMA((1,))],
        input_output_aliases={3: 0},
        compiler_params=_cparams("arbitrary"),
        name="moe_dispatch",
    )(d1.reshape(T // ch, 1, ch), d2.reshape(T // ch, 1, ch), h_pk, jnp.zeros((n_slots, hw), jnp.int32))


def _combine_kernel(d1_ref, d2_ref, n1_ref, n2_ref, ys_ref, wts_ref, x_ref, gate_ref, g_ref, o_ref,
                    buf_a, buf_b, sem, *, ch):
    i = pl.program_id(0)
    slot = i % 2

    def gather(i1_ref, i2_ref, s):
        def issue(k, carry):
            _row_copy(ys_ref, buf_a.at[s], i1_ref[0, k], k, sem.at[s]).start()
            _row_copy(ys_ref, buf_b.at[s], i2_ref[0, k], k, sem.at[s]).start()
            return carry
        lax.fori_loop(0, ch, issue, 0, unroll=8)

    @pl.when(i == 0)
    def _():
        gather(d1_ref, d2_ref, 0)

    @pl.when(i + 1 < pl.num_programs(0))
    def _():
        gather(n1_ref, n2_ref, 1 - slot)

    for buf in (buf_a, buf_b):
        pltpu.make_async_copy(ys_ref.at[pl.ds(0, ch)], buf.at[slot], sem.at[slot]).wait()
    w = wts_ref[...]
    f = w[:, 0:1] * _unpack_halves(buf_a[slot]) + w[:, 1:2] * _unpack_halves(buf_b[slot])
    o_ref[...] = x_ref[...] + gate_ref[...] * _rms(f, g_ref[...])


def _combine(ys, d1, d2, wts, X, modl, g, d, p, n_rows):
    ch = p.ch_dma
    hw = ys.shape[1]
    n = n_rows // ch
    idx_spec = pl.BlockSpec((None, 1, ch), lambda i: (i, 0, 0), memory_space=pltpu.SMEM)
    nxt_spec = pl.BlockSpec((None, 1, ch), lambda i: (jnp.minimum(i + 1, n - 1), 0, 0), memory_space=pltpu.SMEM)
    d1, d2 = d1.reshape(n, 1, ch), d2.reshape(n, 1, ch)
    return pl.pallas_call(
        functools.partial(_combine_kernel, ch=ch),
        grid=(n,),
        in_specs=[
            idx_spec, idx_spec, nxt_spec, nxt_spec,
            pl.BlockSpec(memory_space=pl.ANY),
            pl.BlockSpec((ch, d.E), lambda i: (i, 0)),
            pl.BlockSpec((ch, d.D), lambda i: (i, 0)),
            _mod_spec(5, ch, d, 1),
            pl.BlockSpec((1, d.D), lambda i: (0, 0)),
        ],
        out_specs=pl.BlockSpec((ch, d.D), lambda i: (i, 0)),
        out_shape=jax.ShapeDtypeStruct((n_rows, d.D), F32),
        scratch_shapes=[pltpu.VMEM((2, ch, hw), jnp.int32), pltpu.VMEM((2, ch, hw), jnp.int32),
                        pltpu.SemaphoreType.DMA((2,))],
        compiler_params=_cparams("arbitrary"),
        name="moe_combine",
    )(d1, d2, d1, d2, ys, wts, X, modl, g)


def _moe(X, modl, g_pre, g_post, w_router, w_in, w_out, first_expert, d, p, n_rows):
    tm = p.tm_ffn
    h_pk, info, wts, counts = _router(X, modl, g_pre, w_router, d, p, n_rows)
    counts = counts[0]
    padded = (counts + tm - 1) // tm * tm
    ends = jnp.cumsum(padded)
    offs = ends - padded
    d1 = offs[info[:, 0]] + info[:, 2]
    d2 = offs[info[:, 1]] + info[:, 3]
    n_slots = 2 * n_rows + d.E * tm
    tile_end = ends // tm
    tiles = jnp.arange(n_slots // tm, dtype=jnp.int32)
    tile_expert = jnp.minimum(jnp.sum((tiles[:, None] >= tile_end[None, :]).astype(jnp.int32), axis=1), d.E - 1)
    n_valid = tile_end[-1:].astype(jnp.int32)
    tile_rows = jnp.clip(counts[tile_expert] - (tiles - offs[tile_expert] // tm) * tm, 0, tm)
    tile_rows = jnp.where(tiles < n_valid[0], tile_rows, 0).astype(jnp.int32)
    hs = _dispatch(h_pk, d1, d2, n_slots, d, p)
    ys = _ffn(hs, w_in, w_out, tile_expert + first_expert, n_valid, tile_rows, d, tm, p.tf_exp, packed=True)
    return _combine(ys, d1, d2, wts, X, modl, g_post, d, p, n_rows)


def _dense_ffn(X, modl, g_pre, g_post, w_in, w_out, which, d, p, n_rows):
    tm = p.tm_ffn
    h = _prenorm(X, modl, g_pre, d, p, n_rows)
    n_tiles = n_rows // tm
    f = _ffn(h, w_in, w_out, jnp.full((n_tiles,), which, jnp.int32), jnp.full((1,), n_tiles, jnp.int32),
             jnp.full((n_tiles,), tm, jnp.int32), d, tm, p.tf_ffn, packed=False)
    return _ffn_out(f, X, modl, g_post, d, p, n_rows)


def _rope_tables(d):
    rows = d.S // GRID_W
    t_row = jnp.repeat(jnp.arange(rows, dtype=F32), GRID_W)
    t_col = jnp.tile(jnp.arange(GRID_W, dtype=F32), rows)
    half = HEAD_DIM // 2
    inv = ROPE_THETA ** (-jnp.arange(0, half, 2, dtype=F32) / half)
    ang = jnp.concatenate([t_row[:, None] * inv] * 2 + [t_col[:, None] * inv] * 2, axis=1)
    sign = jnp.tile(jnp.concatenate([-jnp.ones(half // 2, F32), jnp.ones(half // 2, F32)]), 2)
    cos = jnp.tile(jnp.cos(ang), (1, LANES // HEAD_DIM))
    sin = jnp.tile(jnp.sin(ang) * sign, (1, LANES // HEAD_DIM))
    return cos, sin


def _permute_in_cols(w, d):
    a_k, a_v, b_k, b_v, a_q, b_q, g_a, g_b = jnp.split(
        w, [d.A_W, 2 * d.A_W, 2 * d.A_W + d.BKV, 2 * d.A_W + 2 * d.BKV, 3 * d.A_W + 2 * d.BKV,
            3 * d.A_W + 2 * d.BKV + d.B_W, 3 * d.A_W + 2 * d.BKV + d.B_W + d.D], axis=-1)
    return jnp.concatenate([g_a, g_b, a_q, b_q, a_k, b_k, b_v, a_v], axis=-1)


def kernel(x, c, ctx, c_ctx, w_ada, b_ada, norm_g, w_in, diff_lambda, diff_subln_g, sink_logit, w_branch, w_out,
           w_ffn_in, w_ffn_out, w_router, w_exp_in, w_exp_out):
    d = _make_dims(x, ctx, w_ada, w_in, sink_logit, w_branch, w_ffn_out, w_router, w_exp_out)
    p = _make_plan(d)

    X = jnp.concatenate([x.reshape(d.NL, d.D), ctx.reshape(d.NC, d.D)], axis=0)
    cond = jnp.concatenate([c, c_ctx[None, :], jnp.zeros((d.RB - d.B - 1, d.D), F32)], axis=0)
    mod_all = _ada_table(cond, w_ada, b_ada, d, p).reshape(d.depth, d.RB * N_MOD, 1, d.D)
    cos_t, sin_t = _rope_tables(d)

    w_in_b = _permute_in_cols(w_in, d).astype(BF16)
    w_branch_b = w_branch.astype(BF16)
    w_out_b = w_out.astype(BF16)
    w_ffn_in_b = w_ffn_in.astype(BF16)
    w_ffn_out_b = w_ffn_out.astype(BF16)
    w_exp_in_b = w_exp_in.astype(BF16).reshape(-1, d.D, 2 * d.FE)
    w_exp_out_b = w_exp_out.astype(BF16).reshape(-1, d.FE, d.D)

    for l in range(d.depth):
        last = l == d.depth - 1
        lam_init = 0.8 - 0.6 * math.exp(-0.3 * l)
        modl = mod_all[l]
        g = norm_g[l][:, None, :]
        n_rows = d.NL if last else d.NT
        subln = diff_subln_g[l][None, :]

        P = _in_proj(X, modl, g[0], w_in_b, l, d, p, d.NT)
        o_a = _attn_a(P, diff_lambda[l], subln, cos_t, sin_t, lam_init, d, p, latent=True)
        o_b = _attn_b(P, sink_logit[l], cos_t, sin_t, d, p, latent=True)
        ctx_outs = None
        if not last:
            ctx_outs = (_attn_a(P, diff_lambda[l], subln, cos_t, sin_t, lam_init, d, p, latent=False),
                        _attn_b(P, sink_logit[l], cos_t, sin_t, d, p, latent=False))
        X = _merge(o_a, o_b, ctx_outs, P, w_branch_b, w_out_b, l, X, modl, g[1], d, p, n_rows)
        if l % 2 == 0:
            X = _dense_ffn(X, modl, g[2], g[3], w_ffn_in_b, w_ffn_out_b, l // 2, d, p, n_rows)
        else:
            X = _moe(X, modl, g[2], g[3], w_router[l // 2], w_exp_in_b, w_exp_out_b, (l // 2) * d.E, d, p, n_rows)
    return X[:d.NL].reshape(d.B, d.S, d.D)
```

```python
import functools
import math
from typing import NamedTuple

import jax
import jax.numpy as jnp
from jax import lax
from jax.experimental import pallas as pl
from jax.experimental.pallas import tpu as pltpu

HEAD_DIM = 64
GRID_W = 64
WINDOW = 128
BLOCK = 128
N_MOD = 6
ROPE_THETA = 10000.0
EPS = 1e-6
NEG_INF = -1e30
LOG2E = math.log2(math.e)
Q_SCALE = HEAD_DIM ** -0.5 * LOG2E
LANES = 128
SUBLANES = 8
VMEM_LIMIT_BYTES = 56 * 2**20
F32 = jnp.float32
BF16 = jnp.bfloat16
_NT_DIMS = (((1,), (1,)), ((), ()))


class Dims(NamedTuple):
    B: int
    S: int
    C: int
    D: int
    depth: int
    A_W: int
    A_H: int
    B_W: int
    B_H: int
    BKV: int
    G: int
    F: int
    E: int
    FE: int
    IN: int
    NL: int
    NC: int
    NT: int
    RB: int
    c_gA: int
    c_gB: int
    c_Aq: int
    c_Bq: int
    c_Ak: int
    c_Bk: int
    c_Bv: int
    c_Av: int


class Plan(NamedTuple):
    tm_proj: int
    tn_proj: int
    tq_a: int
    tq_b: int
    tm_merge: int
    tm_row: int
    tm_ffn: int
    tf_ffn: int
    tf_exp: int
    ch_dma: int
    tn_ada: int


def _pick(pref, *ns):
    t = pref
    while t > 1 and any(n % t for n in ns):
        t //= 2
    return t


def _make_dims(x, ctx, w_ada, w_in, sink_logit, w_branch, w_ffn_out, w_router, w_exp_out):
    B, S, D = x.shape
    C = ctx.shape[1]
    depth = w_ada.shape[0]
    A_W = w_branch.shape[2]
    B_W = A_W
    A_H = A_W // (2 * HEAD_DIM)
    B_H = sink_logit.shape[1]
    IN = w_in.shape[2]
    BKV = (IN - 3 * A_W - B_W - 2 * D) // 2
    G = B_H // (BKV // HEAD_DIM)
    assert B_H * HEAD_DIM == B_W and G % 2 == 0 and (BKV // HEAD_DIM) % 2 == 0
    assert S % GRID_W == 0 and S >= BLOCK + 2 * WINDOW and S % BLOCK == 0
    c_gA, c_gB, c_Aq = 0, D, 2 * D
    c_Bq = c_Aq + A_W
    c_Ak = c_Bq + B_W
    c_Bk = c_Ak + A_W
    c_Bv = c_Bk + BKV
    c_Av = c_Bv + BKV
    assert c_Av + A_W == IN
    RB = -(-(B + 1) // SUBLANES) * SUBLANES
    return Dims(B, S, C, D, depth, A_W, A_H, B_W, B_H, BKV, G, w_ffn_out.shape[1], w_router.shape[2],
                w_exp_out.shape[2], IN, B * S, B * C, B * S + B * C, RB,
                c_gA, c_gB, c_Aq, c_Bq, c_Ak, c_Bk, c_Bv, c_Av)


def _make_plan(d):
    return Plan(
        tm_proj=_pick(1024, d.S, d.NC),
        tn_proj=_pick(512, d.IN),
        tq_a=_pick(4 * 256, d.S),
        tq_b=_pick(2 * BLOCK, d.S),
        tm_merge=_pick(256, d.S, d.NC),
        tm_row=_pick(512, d.S, d.NC),
        tm_ffn=_pick(1024, d.NL, d.NC),
        tf_ffn=_pick(512, d.F),
        tf_exp=_pick(512, d.FE),
        ch_dma=_pick(512, d.S, d.NC),
        tn_ada=_pick(1024, N_MOD * d.D),
    )


def _cparams(*sem):
    return pltpu.CompilerParams(dimension_semantics=sem, vmem_limit_bytes=VMEM_LIMIT_BYTES)


def _rms(x, g):
    return x * lax.rsqrt(jnp.mean(x * x, axis=-1, keepdims=True) + EPS) * g


def _pack_halves(x):
    n = x.shape[1] // 2
    lo = lax.bitcast_convert_type(x[:, :n].astype(BF16).astype(F32), jnp.int32)
    hi = lax.bitcast_convert_type(x[:, n:].astype(BF16).astype(F32), jnp.int32)
    return lax.shift_right_logical(lo, jnp.int32(16)) | (hi & jnp.int32(-65536))


def _unpack_halves(w):
    lo = lax.bitcast_convert_type(lax.shift_left(w, jnp.int32(16)), F32)
    hi = lax.bitcast_convert_type(w & jnp.int32(-65536), F32)
    return jnp.concatenate([lo, hi], axis=1)


def _rope(x, cos, sin):
    lane = lax.broadcasted_iota(jnp.int32, x.shape, 1)
    partner = jnp.where((lane % 32) < 16, pltpu.roll(x, LANES - 16, 1), pltpu.roll(x, 16, 1))
    return x * cos + partner * sin


def _mod_row(i, tm, d):
    return jnp.where(i < d.NL // tm, (i * tm) // d.S, d.B)


def _mod_spec(k, tm, d, nargs):
    if nargs == 1:
        return pl.BlockSpec((None, 1, d.D), lambda i: (_mod_row(i, tm, d) * N_MOD + k, 0, 0))
    return pl.BlockSpec((None, 1, d.D), lambda i, j: (_mod_row(i, tm, d) * N_MOD + k, 0, 0))


def _ada_kernel(s_ref, w_ref, b_ref, o_ref):
    s = s_ref[...]
    s = (s * jax.nn.sigmoid(s)).astype(BF16)
    o_ref[...] = jnp.dot(s, w_ref[...].astype(BF16), preferred_element_type=F32) + b_ref[...]


def _ada_table(cond, w_ada, b_ada, d, p):
    n6 = N_MOD * d.D
    return pl.pallas_call(
        _ada_kernel,
        grid=(d.depth, n6 // p.tn_ada),
        in_specs=[
            pl.BlockSpec((d.RB, d.D), lambda l, j: (0, 0)),
            pl.BlockSpec((None, d.D, p.tn_ada), lambda l, j: (l, 0, j)),
            pl.BlockSpec((None, 1, p.tn_ada), lambda l, j: (l, 0, j)),
        ],
        out_specs=pl.BlockSpec((None, d.RB, p.tn_ada), lambda l, j: (l, 0, j)),
        out_shape=jax.ShapeDtypeStruct((d.depth, d.RB, n6), F32),
        compiler_params=_cparams("arbitrary", "arbitrary"),
        name="ada_table",
    )(cond, w_ada, b_ada.reshape(d.depth, 1, n6))


def _in_proj_kernel(x_ref, g_ref, shift_ref, scale_ref, w_ref, o_ref, h_scr):
    @pl.when(pl.program_id(1) == 0)
    def _():
        h = _rms(x_ref[...], g_ref[...]) * (1.0 + scale_ref[...]) + shift_ref[...]
        h_scr[...] = h.astype(BF16)

    o_ref[...] = jnp.dot(h_scr[...], w_ref[...], preferred_element_type=F32).astype(o_ref.dtype)


def _in_proj(X, modl, g, w, l, d, p, n_rows):
    tm, tn = p.tm_proj, p.tn_proj
    return pl.pallas_call(
        _in_proj_kernel,
        grid=(n_rows // tm, d.IN // tn),
        in_specs=[
            pl.BlockSpec((tm, d.D), lambda i, j: (i, 0)),
            pl.BlockSpec((1, d.D), lambda i, j: (0, 0)),
            _mod_spec(0, tm, d, 2),
            _mod_spec(1, tm, d, 2),
            pl.BlockSpec((None, d.D, tn), lambda i, j: (l, 0, j)),
        ],
        out_specs=pl.BlockSpec((tm, tn), lambda i, j: (i, j)),
        out_shape=jax.ShapeDtypeStruct((n_rows, d.IN), BF16),
        scratch_shapes=[pltpu.VMEM((tm, d.D), BF16)],
        compiler_params=_cparams("arbitrary", "arbitrary"),
        name="in_proj",
    )(X, g, modl, modl, w)


def _diff_scores(q, k, cos, sin):
    qf = q.astype(F32) * Q_SCALE
    if cos is not None:
        qf = _rope(qf, cos, sin)
    first = lax.broadcasted_iota(jnp.int32, qf.shape, 1) < HEAD_DIM
    q1 = jnp.where(first, qf, 0.0).astype(BF16)
    q2 = jnp.where(first, 0.0, qf).astype(BF16)
    return (lax.dot_general(q1, k, _NT_DIMS, preferred_element_type=F32),
            lax.dot_general(q2, k, _NT_DIMS, preferred_element_type=F32))


def _diff_attend(s1, s2, v, lam, g, lam_init):
    e1 = jnp.exp2(s1 - jnp.max(s1, axis=-1, keepdims=True))
    e2 = jnp.exp2(s2 - jnp.max(s2, axis=-1, keepdims=True))
    l1 = jnp.sum(e1, axis=-1, keepdims=True)
    l2 = jnp.sum(e2, axis=-1, keepdims=True)
    a = (e1 - (lam * l1 / l2) * e2).astype(BF16)
    o = jnp.dot(a, v, preferred_element_type=F32) * (1.0 / l1)
    return _rms(o, g) * (1.0 - lam_init)


def _diff_lambda(lam_ref, lam_init):
    lp = lam_ref[...]
    return (jnp.exp(jnp.sum(lp[0:1, :] * lp[1:2, :], axis=-1, keepdims=True))
            - jnp.exp(jnp.sum(lp[2:3, :] * lp[3:4, :], axis=-1, keepdims=True)) + lam_init)


def _attn_a_latent_kernel(q_ref, kc_ref, vc_ref, kl_ref, vl_ref, cq_ref, sq_ref, ck_ref, sk_ref, lam_ref, g_ref,
                          o_ref, k_scr, v_scr, *, lam_init, n_ctx, n_sub):
    @pl.when(pl.program_id(2) == 0)
    def _():
        k_scr[:n_ctx, :] = kc_ref[...]
        k_scr[n_ctx:, :] = _rope(kl_ref[...].astype(F32), ck_ref[...], sk_ref[...]).astype(BF16)
        v_scr[:n_ctx, :] = vc_ref[...]
        v_scr[n_ctx:, :] = vl_ref[...]

    lam = _diff_lambda(lam_ref, lam_init)
    k, v = k_scr[...], v_scr[...]
    rows = q_ref.shape[0] // n_sub
    subs = [slice(sb * rows, (sb + 1) * rows) for sb in range(n_sub)]
    scores = [_diff_scores(q_ref[sl, :], k, cq_ref[sl, :], sq_ref[sl, :]) for sl in subs]
    for sl, (s1, s2) in zip(subs, scores):
        o_ref[sl, :] = _diff_attend(s1, s2, v, lam, g_ref[...], lam_init).astype(o_ref.dtype)


def _attn_a_context_kernel(q_ref, kc_ref, vc_ref, lam_ref, g_ref, o_ref, *, lam_init):
    s1, s2 = _diff_scores(q_ref[...], kc_ref[...], None, None)
    lam = _diff_lambda(lam_ref, lam_init)
    o_ref[...] = _diff_attend(s1, s2, vc_ref[...], lam, g_ref[...], lam_init).astype(o_ref.dtype)


def _attn_a(P, lam_p, subln_g, cos_t, sin_t, lam_init, d, p, latent):
    lane_blk = lambda c: c // LANES
    qb, kb, vb = lane_blk(d.c_Aq), lane_blk(d.c_Ak), lane_blk(d.c_Av)
    ctx_row = d.NL // d.C
    small = [pl.BlockSpec((4, HEAD_DIM), lambda *_: (0, 0)), pl.BlockSpec((1, 2 * HEAD_DIM), lambda *_: (0, 0))]
    if latent:
        tq = p.tq_a
        nq = d.S // tq
        nkeys = d.C + d.S
        q_tab = pl.BlockSpec((tq, LANES), lambda b, h, i: (i, 0))
        k_tab = pl.BlockSpec((d.S, LANES), lambda b, h, i: (0, 0))
        return pl.pallas_call(
            functools.partial(_attn_a_latent_kernel, lam_init=lam_init, n_ctx=d.C, n_sub=max(tq // 256, 1)),
            grid=(d.B, d.A_H, nq),
            in_specs=[
                pl.BlockSpec((tq, LANES), lambda b, h, i: (b * nq + i, qb + h)),
                pl.BlockSpec((d.C, LANES), lambda b, h, i: (ctx_row + b, kb + h)),
                pl.BlockSpec((d.C, LANES), lambda b, h, i: (ctx_row + b, vb + h)),
                pl.BlockSpec((d.S, LANES), lambda b, h, i: (b, kb + h)),
                pl.BlockSpec((d.S, LANES), lambda b, h, i: (b, vb + h)),
                q_tab, q_tab, k_tab, k_tab,
            ] + small,
            out_specs=pl.BlockSpec((tq, LANES), lambda b, h, i: (b * nq + i, h)),
            out_shape=jax.ShapeDtypeStruct((d.NL, d.A_W), BF16),
            scratch_shapes=[pltpu.VMEM((nkeys, LANES), BF16), pltpu.VMEM((nkeys, LANES), BF16)],
            compiler_params=_cparams("arbitrary", "arbitrary", "arbitrary"),
            name="attn_a_latent",
        )(P, P, P, P, P, cos_t, sin_t, cos_t, sin_t, lam_p, subln_g)
    return pl.pallas_call(
        functools.partial(_attn_a_context_kernel, lam_init=lam_init),
        grid=(d.B, d.A_H),
        in_specs=[
            pl.BlockSpec((d.C, LANES), lambda b, h: (ctx_row + b, qb + h)),
            pl.BlockSpec((d.C, LANES), lambda b, h: (ctx_row + b, kb + h)),
            pl.BlockSpec((d.C, LANES), lambda b, h: (ctx_row + b, vb + h)),
        ] + small,
        out_specs=pl.BlockSpec((d.C, LANES), lambda b, h: (b, h)),
        out_shape=jax.ShapeDtypeStruct((d.NC, d.A_W), BF16),
        compiler_params=_cparams("arbitrary", "arbitrary"),
        name="attn_a_context",
    )(P, P, P, lam_p, subln_g)


def _attn_b_kernel(*refs, has_win, G, S, n_sub):
    if has_win:
        sink_ref, q_ref, kc_ref, vc_ref, kl_ref, vl_ref, cq_ref, sq_ref, ck_ref, sk_ref, o_ref, k_scr = refs

        @pl.when(pl.program_id(2) == 0)
        def _():
            k_scr[...] = _rope(kl_ref[...].astype(F32), ck_ref[...], sk_ref[...]).astype(BF16)
    else:
        sink_ref, q_ref, kc_ref, vc_ref, o_ref = refs
    pair = pl.program_id(1)
    nh = 2 * G
    tq = q_ref.shape[0] // n_sub
    low = lax.broadcasted_iota(jnp.int32, (tq, LANES), 1) < HEAD_DIM
    kc, vc = kc_ref[...], vc_ref[...]

    span = BLOCK + 2 * WINDOW
    subs = [slice(sb * tq, (sb + 1) * tq) for sb in range(n_sub)]
    staged = []
    for sb, sl in enumerate(subs):
        qs = []
        for c in range(G):
            qc = q_ref[sl, c * LANES:(c + 1) * LANES].astype(F32) * Q_SCALE
            if has_win:
                qc = _rope(qc, cq_ref[sl, :], sq_ref[sl, :])
            qr = pltpu.roll(qc, HEAD_DIM, 1)
            if c < G // 2:
                qs += [jnp.where(low, qc, 0.0), jnp.where(low, qr, 0.0)]
            else:
                qs += [jnp.where(low, 0.0, qr), jnp.where(low, 0.0, qc)]
        Q = jnp.concatenate(qs, axis=0).astype(BF16)
        s_c = lax.dot_general(Q, kc, _NT_DIMS, preferred_element_type=F32)
        if has_win:
            blk = pl.program_id(2) * n_sub + sb
            start = pl.multiple_of(jnp.clip(blk * BLOCK - WINDOW, 0, S - span), BLOCK)
            s_w = lax.dot_general(Q, k_scr[pl.ds(start, span), :], _NT_DIMS, preferred_element_type=F32)
            staged.append((s_c, s_w, blk, start))
        else:
            staged.append((s_c, None, None, None))

    for sl, (s_c, s_w, blk, start) in zip(subs, staged):
        if has_win:
            qpos = blk * BLOCK + lax.broadcasted_iota(jnp.int32, (tq, span), 0)
            kpos = start + lax.broadcasted_iota(jnp.int32, (tq, span), 1)
            allowed = jnp.abs(qpos - kpos) <= WINDOW
        p_c, p_w, inv = [], [], []
        for hh in range(nh):
            snk = sink_ref[pair * nh + hh] * LOG2E
            sc = s_c[hh * tq:(hh + 1) * tq]
            m = jnp.maximum(jnp.max(sc, axis=-1, keepdims=True), snk)
            if has_win:
                sw = jnp.where(allowed, s_w[hh * tq:(hh + 1) * tq], NEG_INF)
                m = jnp.maximum(m, jnp.max(sw, axis=-1, keepdims=True))
                ew = jnp.exp2(sw - m)
            ec = jnp.exp2(sc - m)
            den = jnp.sum(ec, axis=-1, keepdims=True) + jnp.exp2(snk - m)
            if has_win:
                den = den + jnp.sum(ew, axis=-1, keepdims=True)
                p_w.append(ew.astype(BF16))
            p_c.append(ec.astype(BF16))
            inv.append(1.0 / den)
        o = jnp.dot(jnp.concatenate(p_c, axis=0), vc, preferred_element_type=F32)
        if has_win:
            vw = vl_ref[pl.ds(start, span), :]
            o = o + jnp.dot(jnp.concatenate(p_w, axis=0), vw, preferred_element_type=F32)

        for c in range(G):
            o_lo = o[(2 * c) * tq:(2 * c + 1) * tq] * inv[2 * c]
            o_hi = o[(2 * c + 1) * tq:(2 * c + 2) * tq] * inv[2 * c + 1]
            if c < G // 2:
                chunk = jnp.where(low, o_lo, pltpu.roll(o_hi, HEAD_DIM, 1))
            else:
                chunk = jnp.where(low, pltpu.roll(o_lo, HEAD_DIM, 1), o_hi)
            o_ref[sl, c * LANES:(c + 1) * LANES] = chunk.astype(o_ref.dtype)


def _attn_b(P, sink, cos_t, sin_t, d, p, latent):
    qw = 2 * d.G * HEAD_DIM
    assert d.c_Bq % qw == 0
    n_pairs = d.BKV // LANES
    qb, kb, vb = d.c_Bq // qw, d.c_Bk // LANES, d.c_Bv // LANES
    ctx_row = d.NL // d.C
    sink_spec = pl.BlockSpec(memory_space=pltpu.SMEM)
    if latent:
        tq = p.tq_b
        nq = d.S // tq
        kern = functools.partial(_attn_b_kernel, has_win=True, G=d.G, S=d.S, n_sub=tq // BLOCK)
        q_tab = pl.BlockSpec((tq, LANES), lambda b, j, i: (i, 0))
        k_tab = pl.BlockSpec((d.S, LANES), lambda b, j, i: (0, 0))
        return pl.pallas_call(
            kern,
            grid=(d.B, n_pairs, nq),
            in_specs=[
                sink_spec,
                pl.BlockSpec((tq, qw), lambda b, j, i: (b * nq + i, qb + j)),
                pl.BlockSpec((d.C, LANES), lambda b, j, i: (ctx_row + b, kb + j)),
                pl.BlockSpec((d.C, LANES), lambda b, j, i: (ctx_row + b, vb + j)),
                pl.BlockSpec((d.S, LANES), lambda b, j, i: (b, kb + j)),
                pl.BlockSpec((d.S, LANES), lambda b, j, i: (b, vb + j)),
                q_tab, q_tab, k_tab, k_tab,
            ],
            out_specs=pl.BlockSpec((tq, qw), lambda b, j, i: (b * nq + i, j)),
            out_shape=jax.ShapeDtypeStruct((d.NL, d.B_W), BF16),
            scratch_shapes=[pltpu.VMEM((d.S, LANES), BF16)],
            compiler_params=_cparams("arbitrary", "arbitrary", "arbitrary"),
            name="attn_b_latent",
        )(sink, P, P, P, P, P, cos_t, sin_t, cos_t, sin_t)
    kern = functools.partial(_attn_b_kernel, has_win=False, G=d.G, S=d.S, n_sub=1)
    return pl.pallas_call(
        kern,
        grid=(d.B, n_pairs),
        in_specs=[
            sink_spec,
            pl.BlockSpec((d.C, qw), lambda b, j: (ctx_row + b, qb + j)),
            pl.BlockSpec((d.C, LANES), lambda b, j: (ctx_row + b, kb + j)),
            pl.BlockSpec((d.C, LANES), lambda b, j: (ctx_row + b, vb + j)),
        ],
        out_specs=pl.BlockSpec((d.C, qw), lambda b, j: (b, j)),
        out_shape=jax.ShapeDtypeStruct((d.NC, d.B_W), BF16),
        compiler_params=_cparams("arbitrary", "arbitrary"),
        name="attn_b_context",
    )(sink, P, P, P)


def _merge_kernel(*refs, nl_tiles, has_ctx):
    if has_ctx:
        oa_ref, ob_ref, oac_ref, obc_ref, ga_ref, gb_ref, wa_ref, wb_ref, wo_ref, x_ref, gate_ref, g_ref, o_ref = refs
        is_lat = pl.program_id(0) < nl_tiles
        oa = jnp.where(is_lat, oa_ref[...], oac_ref[...])
        ob = jnp.where(is_lat, ob_ref[...], obc_ref[...])
    else:
        oa_ref, ob_ref, ga_ref, gb_ref, wa_ref, wb_ref, wo_ref, x_ref, gate_ref, g_ref, o_ref = refs
        oa, ob = oa_ref[...], ob_ref[...]
    ya = jnp.dot(oa, wa_ref[...], preferred_element_type=F32)
    yb = jnp.dot(ob, wb_ref[...], preferred_element_type=F32)
    y = jax.nn.sigmoid(ga_ref[...].astype(F32)) * ya + jax.nn.sigmoid(gb_ref[...].astype(F32)) * yb
    z = jnp.dot(y.astype(BF16), wo_ref[...], preferred_element_type=F32)
    o_ref[...] = x_ref[...] + gate_ref[...] * _rms(z, g_ref[...])


def _merge(o_a, o_b, ctx_outs, P, w_branch, w_out, l, X, modl, g, d, p, n_rows):
    tm = p.tm_merge
    once = pl.Buffered(1)
    nl_tiles = d.NL // tm
    has_ctx = ctx_outs is not None
    lat = lambda i: (jnp.minimum(i, nl_tiles - 1), 0)
    ctx = lambda i: (jnp.maximum(i - nl_tiles, 0), 0)
    mixer_specs = [pl.BlockSpec((tm, d.A_W), lat), pl.BlockSpec((tm, d.B_W), lat)]
    if has_ctx:
        mixer_specs += [pl.BlockSpec((tm, d.A_W), ctx), pl.BlockSpec((tm, d.B_W), ctx)]
    return pl.pallas_call(
        functools.partial(_merge_kernel, nl_tiles=nl_tiles, has_ctx=has_ctx),
        grid=(n_rows // tm,),
        in_specs=mixer_specs + [
            pl.BlockSpec((tm, d.D), lambda i: (i, d.c_gA // d.D)),
            pl.BlockSpec((tm, d.D), lambda i: (i, d.c_gB // d.D)),
            pl.BlockSpec((None, None, d.A_W, d.D), lambda i: (l, 0, 0, 0), pipeline_mode=once),
            pl.BlockSpec((None, None, d.B_W, d.D), lambda i: (l, 1, 0, 0), pipeline_mode=once),
            pl.BlockSpec((None, d.D, d.D), lambda i: (l, 0, 0), pipeline_mode=once),
            pl.BlockSpec((tm, d.D), lambda i: (i, 0)),
            _mod_spec(2, tm, d, 1),
            pl.BlockSpec((1, d.D), lambda i: (0, 0)),
        ],
        out_specs=pl.BlockSpec((tm, d.D), lambda i: (i, 0)),
        out_shape=jax.ShapeDtypeStruct((n_rows, d.D), F32),
        compiler_params=_cparams("arbitrary"),
        name="merge",
    )(o_a, o_b, *(ctx_outs or ()), P, P, w_branch, w_branch, w_out, X, modl, g)


def _prenorm_kernel(x_ref, g_ref, shift_ref, scale_ref, o_ref):
    h = _rms(x_ref[...], g_ref[...]) * (1.0 + scale_ref[...]) + shift_ref[...]
    o_ref[...] = h.astype(o_ref.dtype)


def _prenorm(X, modl, g, d, p, n_rows):
    tm = p.tm_row
    return pl.pallas_call(
        _prenorm_kernel,
        grid=(n_rows // tm,),
        in_specs=[
            pl.BlockSpec((tm, d.D), lambda i: (i, 0)),
            pl.BlockSpec((1, d.D), lambda i: (0, 0)),
            _mod_spec(3, tm, d, 1),
            _mod_spec(4, tm, d, 1),
        ],
        out_specs=pl.BlockSpec((tm, d.D), lambda i: (i, 0)),
        out_shape=jax.ShapeDtypeStruct((n_rows, d.D), BF16),
        compiler_params=_cparams("arbitrary"),
        name="prenorm",
    )(X, g, modl, modl)


def _ffn_kernel(te_ref, nv_ref, tr_ref, h_ref, wg_ref, wu_ref, wo_ref, o_ref, h_scr, acc, *, packed):
    i, j = pl.program_id(0), pl.program_id(1)
    rows = tr_ref[i]
    n_parts = 4
    part = acc.shape[0] // n_parts

    @pl.when((rows > 0) & (j == 0))
    def _():
        h_scr[...] = _unpack_halves(h_ref[...]).astype(BF16) if packed else h_ref[...]
        acc[...] = jnp.zeros_like(acc)

    def accumulate(sl):
        h = h_scr[sl, :]
        gt = jnp.dot(h, wg_ref[...], preferred_element_type=F32)
        up = jnp.dot(h, wu_ref[...], preferred_element_type=F32)
        a = (gt * jax.nn.sigmoid(gt) * up).astype(BF16)
        acc[sl, :] += jnp.dot(a, wo_ref[...], preferred_element_type=F32)

    for q in range(1, n_parts + 1):
        @pl.when((rows > (q - 1) * part) & (rows <= q * part))
        def _(q=q):
            accumulate(slice(0, q * part))

    @pl.when(j == pl.num_programs(1) - 1)
    def _():
        res = jnp.where(rows > 0, acc[...], 0.0)
        o_ref[...] = _pack_halves(res) if packed else res.astype(o_ref.dtype)


def _ffn(hs, w_in, w_out, tile_expert, n_valid, tile_rows, d, tm, tf, packed):
    R = hs.shape[0]
    F = w_out.shape[1]
    nj = F // tf
    hw = hs.shape[1]

    def row(i, nv):
        return jnp.minimum(i, nv[0] - 1)

    def col(i, j, nv):
        return jnp.where(i < nv[0], j, nj - 1)

    grid_spec = pltpu.PrefetchScalarGridSpec(
        num_scalar_prefetch=3,
        grid=(R // tm, nj),
        in_specs=[
            pl.BlockSpec((tm, hw), lambda i, j, te, nv, tr: (row(i, nv), 0)),
            pl.BlockSpec((None, d.D, tf), lambda i, j, te, nv, tr: (te[row(i, nv)], 0, col(i, j, nv))),
            pl.BlockSpec((None, d.D, tf), lambda i, j, te, nv, tr: (te[row(i, nv)], 0, nj + col(i, j, nv))),
            pl.BlockSpec((None, tf, d.D), lambda i, j, te, nv, tr: (te[row(i, nv)], col(i, j, nv), 0)),
        ],
        out_specs=pl.BlockSpec((tm, hw), lambda i, j, te, nv, tr: (i, 0)),
        scratch_shapes=[pltpu.VMEM((tm, d.D), BF16), pltpu.VMEM((tm, d.D), F32)],
    )
    return pl.pallas_call(
        functools.partial(_ffn_kernel, packed=packed),
        grid_spec=grid_spec,
        out_shape=jax.ShapeDtypeStruct((R, hw), hs.dtype),
        compiler_params=_cparams("arbitrary", "arbitrary"),
        name="ffn_packed" if packed else "ffn_dense",
    )(tile_expert, n_valid, tile_rows, hs, w_in, w_in, w_out)


def _ffn_out_kernel(f_ref, x_ref, gate_ref, g_ref, o_ref):
    o_ref[...] = x_ref[...] + gate_ref[...] * _rms(f_ref[...].astype(F32), g_ref[...])


def _ffn_out(f, X, modl, g, d, p, n_rows):
    tm = p.tm_row
    return pl.pallas_call(
        _ffn_out_kernel,
        grid=(n_rows // tm,),
        in_specs=[
            pl.BlockSpec((tm, d.D), lambda i: (i, 0)),
            pl.BlockSpec((tm, d.D), lambda i: (i, 0)),
            _mod_spec(5, tm, d, 1),
            pl.BlockSpec((1, d.D), lambda i: (0, 0)),
        ],
        out_specs=pl.BlockSpec((tm, d.D), lambda i: (i, 0)),
        out_shape=jax.ShapeDtypeStruct((n_rows, d.D), F32),
        compiler_params=_cparams("arbitrary"),
        name="ffn_out",
    )(f, X, modl, g)


def _router_kernel(x_ref, g_ref, shift_ref, scale_ref, wr_ref, hp_ref, info_ref, wts_ref, cnt_ref, base):
    i = pl.program_id(0)

    @pl.when(i == 0)
    def _():
        base[...] = jnp.zeros_like(base)

    h = _rms(x_ref[...], g_ref[...]) * (1.0 + scale_ref[...]) + shift_ref[...]
    hp_ref[...] = _pack_halves(h)
    logits = jnp.dot(h, wr_ref[...], preferred_element_type=F32)
    tm, E = logits.shape
    lane = lax.broadcasted_iota(jnp.int32, (tm, E), 1).astype(F32)
    m1 = jnp.max(logits, axis=-1, keepdims=True)
    i1 = jnp.min(jnp.where(logits == m1, lane, float(E)), axis=-1, keepdims=True)
    rest = jnp.where(lane == i1, -jnp.inf, logits)
    m2 = jnp.max(rest, axis=-1, keepdims=True)
    i2 = jnp.min(jnp.where(rest == m2, lane, float(E)), axis=-1, keepdims=True)
    e2 = jnp.exp(m2 - m1)
    w1 = 1.0 / (1.0 + e2)
    w2 = e2 * w1

    oh1 = (lane == i1).astype(F32)
    oh2 = (lane == i2).astype(F32)
    oh = oh1 + oh2
    r_i = lax.broadcasted_iota(jnp.int32, (tm, tm), 0)
    c_i = lax.broadcasted_iota(jnp.int32, (tm, tm), 1)
    earlier = (c_i < r_i).astype(BF16)
    before = base[...] + jnp.dot(earlier, oh.astype(BF16), preferred_element_type=F32)
    rank1 = jnp.sum(oh1 * before, axis=-1, keepdims=True)
    rank2 = jnp.sum(oh2 * before, axis=-1, keepdims=True)
    new_base = base[...] + jnp.sum(oh, axis=0, keepdims=True)
    base[...] = new_base
    cnt_ref[...] = new_base.astype(jnp.int32)

    col = lax.broadcasted_iota(jnp.int32, (tm, E), 1)
    info = jnp.where(col == 0, i1, jnp.where(col == 1, i2, jnp.where(col == 2, rank1, jnp.where(col == 3, rank2, 0.0))))
    info_ref[...] = info.astype(jnp.int32)
    wts_ref[...] = jnp.where(col == 0, w1, jnp.where(col == 1, w2, 0.0))


def _router(X, modl, g, w_router, d, p, n_rows):
    tm = p.tm_row
    E = d.E
    return pl.pallas_call(
        _router_kernel,
        grid=(n_rows // tm,),
        in_specs=[
            pl.BlockSpec((tm, d.D), lambda i: (i, 0)),
            pl.BlockSpec((1, d.D), lambda i: (0, 0)),
            _mod_spec(3, tm, d, 1),
            _mod_spec(4, tm, d, 1),
            pl.BlockSpec((d.D, E), lambda i: (0, 0)),
        ],
        out_specs=[
            pl.BlockSpec((tm, d.D // 2), lambda i: (i, 0)),
            pl.BlockSpec((tm, E), lambda i: (i, 0)),
            pl.BlockSpec((tm, E), lambda i: (i, 0)),
            pl.BlockSpec((1, E), lambda i: (0, 0)),
        ],
        out_shape=[
            jax.ShapeDtypeStruct((n_rows, d.D // 2), jnp.int32),
            jax.ShapeDtypeStruct((n_rows, E), jnp.int32),
            jax.ShapeDtypeStruct((n_rows, E), F32),
            jax.ShapeDtypeStruct((1, E), jnp.int32),
        ],
        scratch_shapes=[pltpu.VMEM((1, E), F32)],
        compiler_params=_cparams("arbitrary"),
        name="router",
    )(X, g, modl, modl, w_router)


def _row_copy(src, dst, s, t, sem):
    return pltpu.make_async_copy(src.at[pl.ds(s, 1)], dst.at[pl.ds(t, 1)], sem)


def _dispatch_kernel(d1_ref, d2_ref, h_ref, init_ref, o_ref, sem, *, ch):
    del init_ref

    def issue(k, carry):
        _row_copy(h_ref, o_ref, k, d1_ref[0, k], sem.at[0]).start(priority=0)
        _row_copy(h_ref, o_ref, k, d2_ref[0, k], sem.at[0]).start(priority=1)
        return carry

    lax.fori_loop(0, ch, issue, 0, unroll=8)
    for _ in range(2):
        pltpu.make_async_copy(h_ref, o_ref.at[pl.ds(0, ch)], sem.at[0]).wait()


def _dispatch(h_pk, d1, d2, n_slots, d, p):
    T, hw = h_pk.shape
    ch = p.ch_dma
    idx_spec = pl.BlockSpec((None, 1, ch), lambda i: (i, 0, 0), memory_space=pltpu.SMEM)
    any_spec = pl.BlockSpec(memory_space=pl.ANY)
    return pl.pallas_call(
        functools.partial(_dispatch_kernel, ch=ch),
        grid=(T // ch,),
        in_specs=[idx_spec, idx_spec, pl.BlockSpec((ch, hw), lambda i: (i, 0)), any_spec],
        out_specs=any_spec,
        out_shape=jax.ShapeDtypeStruct((n_slots, hw), jnp.int32),
        scratch_shapes=[pltpu.SemaphoreType.DMA((1,))],
        input_output_aliases={3: 0},
        compiler_params=_cparams("arbitrary"),
        name="moe_dispatch",
    )(d1.reshape(T // ch, 1, ch), d2.reshape(T // ch, 1, ch), h_pk, jnp.zeros((n_slots, hw), jnp.int32))


def _combine_kernel(d1_ref, d2_ref, n1_ref, n2_ref, ys_ref, wts_ref, x_ref, gate_ref, g_ref, o_ref,
                    buf_a, buf_b, sem, *, ch):
    i = pl.program_id(0)
    slot = i % 2

    def gather(i1_ref, i2_ref, s):
        def issue(k, carry):
            _row_copy(ys_ref, buf_a.at[s], i1_ref[0, k], k, sem.at[s]).start(priority=0)
            _row_copy(ys_ref, buf_b.at[s], i2_ref[0, k], k, sem.at[s]).start(priority=1)
            return carry
        lax.fori_loop(0, ch, issue, 0, unroll=8)

    @pl.when(i == 0)
    def _():
        gather(d1_ref, d2_ref, 0)

    @pl.when(i + 1 < pl.num_programs(0))
    def _():
        gather(n1_ref, n2_ref, 1 - slot)

    for buf in (buf_a, buf_b):
        pltpu.make_async_copy(ys_ref.at[pl.ds(0, ch)], buf.at[slot], sem.at[slot]).wait()
    w = wts_ref[...]
    f = w[:, 0:1] * _unpack_halves(buf_a[slot]) + w[:, 1:2] * _unpack_halves(buf_b[slot])
    o_ref[...] = x_ref[...] + gate_ref[...] * _rms(f, g_ref[...])


def _combine(ys, d1, d2, wts, X, modl, g, d, p, n_rows):
    ch = p.ch_dma
    hw = ys.shape[1]
    n = n_rows // ch
    idx_spec = pl.BlockSpec((None, 1, ch), lambda i: (i, 0, 0), memory_space=pltpu.SMEM)
    nxt_spec = pl.BlockSpec((None, 1, ch), lambda i: (jnp.minimum(i + 1, n - 1), 0, 0), memory_space=pltpu.SMEM)
    d1, d2 = d1.reshape(n, 1, ch), d2.reshape(n, 1, ch)
    return pl.pallas_call(
        functools.partial(_combine_kernel, ch=ch),
        grid=(n,),
        in_specs=[
            idx_spec, idx_spec, nxt_spec, nxt_spec,
            pl.BlockSpec(memory_space=pl.ANY),
            pl.BlockSpec((ch, d.E), lambda i: (i, 0)),
            pl.BlockSpec((ch, d.D), lambda i: (i, 0)),
            _mod_spec(5, ch, d, 1),
            pl.BlockSpec((1, d.D), lambda i: (0, 0)),
        ],
        out_specs=pl.BlockSpec((ch, d.D), lambda i: (i, 0)),
        out_shape=jax.ShapeDtypeStruct((n_rows, d.D), F32),
        scratch_shapes=[pltpu.VMEM((2, ch, hw), jnp.int32), pltpu.VMEM((2, ch, hw), jnp.int32),
                        pltpu.SemaphoreType.DMA((2,))],
        compiler_params=_cparams("arbitrary"),
        name="moe_combine",
    )(d1, d2, d1, d2, ys, wts, X, modl, g)


def _moe(X, modl, g_pre, g_post, w_router, w_in, w_out, first_expert, d, p, n_rows):
    tm = p.tm_ffn
    h_pk, info, wts, counts = _router(X, modl, g_pre, w_router, d, p, n_rows)
    counts = counts[0]
    padded = (counts + tm - 1) // tm * tm
    ends = jnp.cumsum(padded)
    offs = ends - padded
    d1 = offs[info[:, 0]] + info[:, 2]
    d2 = offs[info[:, 1]] + info[:, 3]
    n_slots = 2 * n_rows + d.E * tm
    tile_end = ends // tm
    tiles = jnp.arange(n_slots // tm, dtype=jnp.int32)
    tile_expert = jnp.minimum(jnp.sum((tiles[:, None] >= tile_end[None, :]).astype(jnp.int32), axis=1), d.E - 1)
    n_valid = tile_end[-1:].astype(jnp.int32)
    tile_rows = jnp.clip(counts[tile_expert] - (tiles - offs[tile_expert] // tm) * tm, 0, tm)
    tile_rows = jnp.where(tiles < n_valid[0], tile_rows, 0).astype(jnp.int32)
    hs = _dispatch(h_pk, d1, d2, n_slots, d, p)
    ys = _ffn(hs, w_in, w_out, tile_expert + first_expert, n_valid, tile_rows, d, tm, p.tf_exp, packed=True)
    return _combine(ys, d1, d2, wts, X, modl, g_post, d, p, n_rows)


def _dense_ffn(X, modl, g_pre, g_post, w_in, w_out, which, d, p, n_rows):
    tm = p.tm_ffn
    h = _prenorm(X, modl, g_pre, d, p, n_rows)
    n_tiles = n_rows // tm
    f = _ffn(h, w_in, w_out, jnp.full((n_tiles,), which, jnp.int32), jnp.full((1,), n_tiles, jnp.int32),
             jnp.full((n_tiles,), tm, jnp.int32), d, tm, p.tf_ffn, packed=False)
    return _ffn_out(f, X, modl, g_post, d, p, n_rows)


def _rope_tables(d):
    rows = d.S // GRID_W
    t_row = jnp.repeat(jnp.arange(rows, dtype=F32), GRID_W)
    t_col = jnp.tile(jnp.arange(GRID_W, dtype=F32), rows)
    half = HEAD_DIM // 2
    inv = ROPE_THETA ** (-jnp.arange(0, half, 2, dtype=F32) / half)
    ang = jnp.concatenate([t_row[:, None] * inv] * 2 + [t_col[:, None] * inv] * 2, axis=1)
    sign = jnp.tile(jnp.concatenate([-jnp.ones(half // 2, F32), jnp.ones(half // 2, F32)]), 2)
    cos = jnp.tile(jnp.cos(ang), (1, LANES // HEAD_DIM))
    sin = jnp.tile(jnp.sin(ang) * sign, (1, LANES // HEAD_DIM))
    return cos, sin


def _permute_in_cols(w, d):
    a_k, a_v, b_k, b_v, a_q, b_q, g_a, g_b = jnp.split(
        w, [d.A_W, 2 * d.A_W, 2 * d.A_W + d.BKV, 2 * d.A_W + 2 * d.BKV, 3 * d.A_W + 2 * d.BKV,
            3 * d.A_W + 2 * d.BKV + d.B_W, 3 * d.A_W + 2 * d.BKV + d.B_W + d.D], axis=-1)
    return jnp.concatenate([g_a, g_b, a_q, b_q, a_k, b_k, b_v, a_v], axis=-1)


def kernel(x, c, ctx, c_ctx, w_ada, b_ada, norm_g, w_in, diff_lambda, diff_subln_g, sink_logit, w_branch, w_out,
           w_ffn_in, w_ffn_out, w_router, w_exp_in, w_exp_out):
    d = _make_dims(x, ctx, w_ada, w_in, sink_logit, w_branch, w_ffn_out, w_router, w_exp_out)
    p = _make_plan(d)

    X = jnp.concatenate([x.reshape(d.NL, d.D), ctx.reshape(d.NC, d.D)], axis=0)
    cond = jnp.concatenate([c, c_ctx[None, :], jnp.zeros((d.RB - d.B - 1, d.D), F32)], axis=0)
    mod_all = _ada_table(cond, w_ada, b_ada, d, p).reshape(d.depth, d.RB * N_MOD, 1, d.D)
    cos_t, sin_t = _rope_tables(d)

    w_in_b = _permute_in_cols(w_in, d).astype(BF16)
    w_branch_b = w_branch.astype(BF16)
    w_out_b = w_out.astype(BF16)
    w_ffn_in_b = w_ffn_in.astype(BF16)
    w_ffn_out_b = w_ffn_out.astype(BF16)
    w_exp_in_b = w_exp_in.astype(BF16).reshape(-1, d.D, 2 * d.FE)
    w_exp_out_b = w_exp_out.astype(BF16).reshape(-1, d.FE, d.D)

    for l in range(d.depth):
        last = l == d.depth - 1
        lam_init = 0.8 - 0.6 * math.exp(-0.3 * l)
        modl = mod_all[l]
        g = norm_g[l][:, None, :]
        n_rows = d.NL if last else d.NT
        subln = diff_subln_g[l][None, :]

        P = _in_proj(X, modl, g[0], w_in_b, l, d, p, d.NT)
        o_a = _attn_a(P, diff_lambda[l], subln, cos_t, sin_t, lam_init, d, p, latent=True)
        o_b = _attn_b(P, sink_logit[l], cos_t, sin_t, d, p, latent=True)
        ctx_outs = None
        if not last:
            ctx_outs = (_attn_a(P, diff_lambda[l], subln, cos_t, sin_t, lam_init, d, p, latent=False),
                        _attn_b(P, sink_logit[l], cos_t, sin_t, d, p, latent=False))
        X = _merge(o_a, o_b, ctx_outs, P, w_branch_b, w_out_b, l, X, modl, g[1], d, p, n_rows)
        if l % 2 == 0:
            X = _dense_ffn(X, modl, g[2], g[3], w_ffn_in_b, w_ffn_out_b, l // 2, d, p, n_rows)
        else:
            X = _moe(X, modl, g[2], g[3], w_router[l // 2], w_exp_in_b, w_exp_out_b, (l // 2) * d.E, d, p, n_rows)
    return X[:d.NL].reshape(d.B, d.S, d.D)
```
